```python
import math
import jax, jax.numpy as jnp
from jax import lax
import numpy as np

D_MODEL = 2048
BATCH = 1
SEQ = 8192
DEPTH = 4

N_A = DEPTH // 2
N_B = DEPTH - N_A

HEAD_DIM = 128
DN_HEADS = 12
DN_WIDTH = DN_HEADS * HEAD_DIM
CONV_K = 4
CHUNK = 64
DT_MIN = 0.001
DT_MAX = 0.1

MEM_LEN = 256
MEM_HEADS = 4
MEM_WIDTH = MEM_HEADS * HEAD_DIM

DIL_GROUPS = ((128, 1), (512, 4), (2048, 16))
DIL_HEADS_PER_GROUP = 4
DIL_HEADS = DIL_HEADS_PER_GROUP * len(DIL_GROUPS)
DIL_WIDTH = DIL_HEADS * HEAD_DIM
DIL_OUT = DIL_HEADS_PER_GROUP * HEAD_DIM
DIL_BLOCK = 128

D_FF = ((8 * D_MODEL // 3 + 127) // 128) * 128
A_IN = 4 * DN_WIDTH + 2 * DN_HEADS + MEM_WIDTH
B_IN = DIL_WIDTH + MEM_WIDTH
NORM_EPS = 1e-6

kernel_name = 'yoco_deltanet_dilated_hybrid'


def rmsnorm(x, gain):
    x32 = x.astype(jnp.float32)
    y = x32 * lax.rsqrt(jnp.mean(x32 * x32, axis=-1, keepdims=True) + NORM_EPS)
    return (y * gain.astype(jnp.float32)).astype(x.dtype)


def l2norm(x):
    return x * lax.rsqrt(jnp.sum(x * x, axis=-1, keepdims=True) + NORM_EPS)


def swiglu(x, w_in, w_out):
    gate, up = jnp.split(x @ w_in, 2, axis=-1)
    return (jax.nn.silu(gate) * up) @ w_out


def causal_depthwise_conv(x, w):
    c = x.shape[-1]
    return lax.conv_general_dilated(
        x, w[:, None, :].astype(x.dtype), window_strides=(1,), padding=[(CONV_K - 1, 0)],
        dimension_numbers=('NWC', 'WIO', 'NWC'), feature_group_count=c)


def memory_kv(mem, gain, w):
    b, m, _ = mem.shape
    k, v = jnp.split(rmsnorm(mem, gain) @ w, 2, axis=-1)
    return k.reshape(b, m, MEM_HEADS, HEAD_DIM), v.reshape(b, m, MEM_HEADS, HEAD_DIM)


def memory_attention(q, mem_k, mem_v):
    s = jnp.einsum('bshd,bmhd->bhsm', q.astype(jnp.float32), mem_k.astype(jnp.float32)) * (HEAD_DIM ** -0.5)
    p = jax.nn.softmax(s, axis=-1)
    return jnp.einsum('bhsm,bmhd->bshd', p, mem_v.astype(jnp.float32)).astype(q.dtype)


def chunk_gated_delta_rule(q, k, v, g, beta):
    b_, s_, h_, d = q.shape
    n = s_ // CHUNK

    def to_chunks(t):
        t = t.reshape((b_, n, CHUNK) + t.shape[2:])
        return jnp.moveaxis(t, 2, 3)

    q, k, v = to_chunks(q), to_chunks(k), to_chunks(v)
    g, beta = to_chunks(g), to_chunks(beta)
    g = jnp.cumsum(g, axis=-1)
    idx = jnp.arange(CHUNK)
    causal = idx[:, None] >= idx[None, :]
    strict = idx[:, None] > idx[None, :]
    decay = jnp.exp(jnp.where(causal, g[..., :, None] - g[..., None, :], -jnp.inf))
    k_beta = k * beta[..., None]
    lower = jnp.where(strict, jnp.einsum('bnhid,bnhjd->bnhij', k_beta, k) * decay, 0.0)
    eye = jnp.eye(CHUNK, dtype=q.dtype)
    rhs = jnp.concatenate([v * beta[..., None], k_beta * jnp.exp(g)[..., None]], axis=-1)
    sol = lax.linalg.triangular_solve(eye + lower, rhs, left_side=True, lower=True)
    u, w = sol[..., :d], sol[..., d:]
    attn = jnp.einsum('bnhid,bnhjd->bnhij', q, k) * decay
    q_decay = q * jnp.exp(g)[..., None]
    k_decay = k * jnp.exp(g[..., -1:] - g)[..., None]
    g_last = jnp.exp(g[..., -1])
    xs = tuple(jnp.moveaxis(t, 1, 0) for t in (q_decay, k_decay, u, w, attn, g_last))

    def step(state, inp):
        qd, kd, u_c, w_c, a_c, gl = inp
        v_new = u_c - jnp.einsum('bhcd,bhde->bhce', w_c, state)
        o = jnp.einsum('bhcd,bhde->bhce', qd, state) + jnp.einsum('bhij,bhje->bhie', a_c, v_new)
        state = state * gl[..., None, None] + jnp.einsum('bhcd,bhce->bhde', kd, v_new)
        return state, o

    state0 = jnp.zeros((b_, h_, d, d), q.dtype)
    _, o = lax.scan(step, state0, xs)
    return jnp.moveaxis(o, 0, 1).transpose(0, 1, 3, 2, 4).reshape(b_, s_, h_, d)


def deltanet_mixer(h, mem_k, mem_v, w_in, conv_w, a_log, dt_bias, o_norm, w_out):
    b_, s_, _ = h.shape
    f32 = jnp.float32
    splits = np.cumsum([3 * DN_WIDTH, DN_WIDTH, DN_HEADS, DN_HEADS]).tolist()
    qkv, z, a, bb, mq = jnp.split(h @ w_in, splits, axis=-1)
    qkv = jax.nn.silu(causal_depthwise_conv(qkv, conv_w))
    q, k, v = [t.reshape(b_, s_, DN_HEADS, HEAD_DIM).astype(f32) for t in jnp.split(qkv, 3, axis=-1)]
    q = l2norm(q) * (HEAD_DIM ** -0.5)
    k = l2norm(k)
    beta = jax.nn.sigmoid(bb.astype(f32))
    g = -jnp.exp(a_log.astype(f32)) * jax.nn.softplus(a.astype(f32) + dt_bias.astype(f32))
    o = chunk_gated_delta_rule(q, k, v, g, beta)
    o = rmsnorm(o, o_norm) * jax.nn.silu(z.reshape(b_, s_, DN_HEADS, HEAD_DIM).astype(f32))
    o_mem = memory_attention(mq.reshape(b_, s_, MEM_HEADS, HEAD_DIM), mem_k, mem_v)
    y = jnp.concatenate([o.reshape(b_, s_, DN_WIDTH).astype(h.dtype), o_mem.reshape(b_, s_, MEM_WIDTH)], axis=-1)
    return y @ w_out


def alibi_slopes(n_heads):
    return jnp.exp2(-8.0 * jnp.arange(1, n_heads + 1, dtype=jnp.float32) / n_heads)


def dilated_group_attention(q, k, v, dil, steps, slopes):
    b_, s_, h_, d = q.shape
    f32 = jnp.float32
    L = s_ // dil
    nb = -(-L // DIL_BLOCK)
    lp = nb * DIL_BLOCK

    def streams(t):
        t = t.astype(f32).reshape(b_, L, dil, h_, d).transpose(0, 2, 1, 3, 4)
        t = jnp.pad(t, ((0, 0), (0, 0), (0, lp - L), (0, 0), (0, 0)))
        return t.reshape(b_, dil, nb, DIL_BLOCK, h_, d)

    def with_prev(t):
        prev = jnp.pad(t, ((0, 0), (0, 0), (1, 0), (0, 0), (0, 0), (0, 0)))[:, :, :nb]
        return jnp.concatenate([prev, t], axis=3)

    qb = streams(q)
    kc, vc = with_prev(streams(k)), with_prev(streams(v))
    s = jnp.einsum('bcnqhd,bcnkhd->bcnhqk', qb, kc) * (d ** -0.5)
    qi = jnp.arange(DIL_BLOCK)[:, None] + DIL_BLOCK
    kj = jnp.arange(2 * DIL_BLOCK)[None, :]
    rel = qi - kj
    key_ok = (jnp.arange(nb)[:, None] * DIL_BLOCK - DIL_BLOCK + kj) >= 0
    valid = ((rel >= 0) & (rel <= steps))[None] & key_ok[:, None, :]
    bias = -slopes[:, None, None] * (rel * dil).astype(f32)
    s = jnp.where(valid[None, None, :, None], s + bias, -jnp.inf)
    m = jnp.max(s, axis=-1, keepdims=True)
    e = jnp.exp(s - m)
    den = jnp.sum(e, axis=-1, keepdims=True)
    o = jnp.einsum('bcnhqk,bcnkhd->bcnqhd', e / den, vc)
    lse = (m + jnp.log(den))[..., 0]
    o = o.reshape(b_, dil, lp, h_, d)[:, :, :L].transpose(0, 2, 1, 3, 4).reshape(b_, s_, h_, d)
    lse = lse.transpose(0, 1, 2, 4, 3).reshape(b_, dil, lp, h_)[:, :, :L].transpose(0, 2, 1, 3).reshape(b_, s_, h_)
    return o, lse


def shared_kv(x, gain, w_kv):
    b_, s_, _ = x.shape
    k, v = jnp.split(rmsnorm(x, gain) @ w_kv, 2, axis=-1)
    return k.reshape(b_, s_, DIL_HEADS, HEAD_DIM), v.reshape(b_, s_, DIL_HEADS, HEAD_DIM)


def dilated_mixer(h, k_sh, v_sh, mem_k, mem_v, w_in, w_out):
    b_, s_, _ = h.shape
    q, mq = jnp.split(h @ w_in, [DIL_WIDTH], axis=-1)
    q = q.reshape(b_, s_, DIL_HEADS, HEAD_DIM)
    slopes = alibi_slopes(DIL_HEADS)
    outs, lses = [], []
    for gi, (win, dil) in enumerate(DIL_GROUPS):
        hs = slice(gi * DIL_HEADS_PER_GROUP, (gi + 1) * DIL_HEADS_PER_GROUP)
        o, lse = dilated_group_attention(q[:, :, hs], k_sh[:, :, hs], v_sh[:, :, hs], dil, win // dil, slopes[hs])
        outs.append(o)
        lses.append(lse)
    wts = jax.nn.softmax(jnp.stack(lses), axis=0)
    o = jnp.einsum('gbsh,gbshd->bshd', wts, jnp.stack(outs)).astype(h.dtype)
    o_mem = memory_attention(mq.reshape(b_, s_, MEM_HEADS, HEAD_DIM), mem_k, mem_v)
    y = jnp.concatenate([o.reshape(b_, s_, DIL_OUT), o_mem.reshape(b_, s_, MEM_WIDTH)], axis=-1)
    return y @ w_out


def setup_inputs(seed: int = 0) -> dict:
    key = jax.random.key(seed)
    ks = jax.random.split(key, 20)
    f32 = jnp.float32

    def normal(k, shape, scale):
        return jax.random.normal(k, shape, f32) * scale

    def gain(k, shape):
        return 1.0 + 0.02 * jax.random.normal(k, shape, f32)

    u = jax.random.uniform(ks[9], (N_A, DN_HEADS), f32)
    dt = jnp.exp(u * (math.log(DT_MAX) - math.log(DT_MIN)) + math.log(DT_MIN))
    return {
        'x': normal(ks[0], (BATCH, SEQ, D_MODEL), 1.0),
        'mem': normal(ks[1], (BATCH, MEM_LEN, D_MODEL), 1.0),
        'norm_gains': gain(ks[2], (DEPTH, 6, D_MODEL)),
        'ffn_w_in': normal(ks[3], (DEPTH, 2, D_MODEL, 2 * D_FF), D_MODEL ** -0.5),
        'ffn_w_out': normal(ks[4], (DEPTH, 2, D_FF, D_MODEL), D_FF ** -0.5),
        'mem_norm_gain': gain(ks[5], (DEPTH, D_MODEL)),
        'w_mem_kv': normal(ks[6], (DEPTH, D_MODEL, 2 * MEM_WIDTH), D_MODEL ** -0.5),
        'dn_w_in': normal(ks[7], (N_A, D_MODEL, A_IN), D_MODEL ** -0.5),
        'dn_conv': normal(ks[8], (N_A, CONV_K, 3 * DN_WIDTH), CONV_K ** -0.5),
        'dn_a_log': jnp.log(jax.random.uniform(ks[10], (N_A, DN_HEADS), f32, 1.0, 16.0)),
        'dn_dt_bias': dt + jnp.log(-jnp.expm1(-dt)),
        'dn_o_norm': gain(ks[11], (N_A, HEAD_DIM)),
        'dn_w_out': normal(ks[12], (N_A, DN_WIDTH + MEM_WIDTH, D_MODEL), (DN_WIDTH + MEM_WIDTH) ** -0.5),
        'kv_norm_gain': gain(ks[13], (D_MODEL,)),
        'w_kv': normal(ks[14], (D_MODEL, 2 * DIL_WIDTH), D_MODEL ** -0.5),
        'dil_w_in': normal(ks[15], (N_B, D_MODEL, B_IN), D_MODEL ** -0.5),
        'dil_w_out': normal(ks[16], (N_B, DIL_OUT + MEM_WIDTH, D_MODEL), (DIL_OUT + MEM_WIDTH) ** -0.5),
    }


def reference(x, mem, norm_gains, ffn_w_in, ffn_w_out, mem_norm_gain, w_mem_kv,
              dn_w_in, dn_conv, dn_a_log, dn_dt_bias, dn_o_norm, dn_w_out,
              kv_norm_gain, w_kv, dil_w_in, dil_w_out):
    k_sh, v_sh = None, None
    for l in range(DEPTH):
        gains = norm_gains[l]
        if l == N_A:
            k_sh, v_sh = shared_kv(x, kv_norm_gain, w_kv)
        x = x + 0.5 * rmsnorm(swiglu(rmsnorm(x, gains[0]), ffn_w_in[l, 0], ffn_w_out[l, 0]), gains[1])
        mem_k, mem_v = memory_kv(mem, mem_norm_gain[l], w_mem_kv[l])
        h = rmsnorm(x, gains[2])
        if l < N_A:
            y = deltanet_mixer(h, mem_k, mem_v, dn_w_in[l], dn_conv[l], dn_a_log[l],
                               dn_dt_bias[l], dn_o_norm[l], dn_w_out[l])
        else:
            i = l - N_A
            y = dilated_mixer(h, k_sh, v_sh, mem_k, mem_v, dil_w_in[i], dil_w_out[i])
        x = x + rmsnorm(y, gains[3])
        x = x + 0.5 * rmsnorm(swiglu(rmsnorm(x, gains[4]), ffn_w_in[l, 1], ffn_w_out[l, 1]), gains[5])
    return x
```

```python
import functools
import math

import jax
import jax.numpy as jnp
from jax import lax
from jax.experimental import pallas as pl
from jax.experimental.pallas import tpu as pltpu

F32 = jnp.float32
BF16 = jnp.bfloat16

D_MODEL = 2048
HEAD_DIM = 128
DN_HEADS = 12
DN_WIDTH = DN_HEADS * HEAD_DIM
CONV_K = 4
CHUNK = 64
MEM_HEADS = 4
MEM_WIDTH = MEM_HEADS * HEAD_DIM
DIL_GROUPS = ((128, 1), (512, 4), (2048, 16))
DIL_HEADS_PER_GROUP = 4
DIL_HEADS = DIL_HEADS_PER_GROUP * len(DIL_GROUPS)
DIL_WIDTH = DIL_HEADS * HEAD_DIM
DIL_OUT = DIL_HEADS_PER_GROUP * HEAD_DIM
DIL_BLOCK = 128
D_FF = ((8 * D_MODEL // 3 + 127) // 128) * 128
NORM_EPS = 1e-6

V7X_LANES = 128
V7X_BF16_SUBLANES = 16
V7X_VMEM_LIMIT_BYTES = 56 * 1024 * 1024

FF_TILE = 512
D_FF_PAD = -(-D_FF // FF_TILE) * FF_TILE
GATE_ROWS = 16
SUPER = 2 * CHUNK


def _cparams(sem):
    return pltpu.CompilerParams(dimension_semantics=sem, vmem_limit_bytes=V7X_VMEM_LIMIT_BYTES)


def _rms_scale(x):
    return lax.rsqrt(jnp.mean(x * x, axis=-1, keepdims=True) + NORM_EPS)


def _silu(x):
    return x * jax.nn.sigmoid(x)


def _block_id(idx, size):
    assert size & (size - 1) == 0
    return lax.shift_right_logical(idx, size.bit_length() - 1)


def _dot(a, b):
    return jnp.dot(a, b, preferred_element_type=F32)


def _dot_nt(a, b):
    return lax.dot_general(a, b, (((1,), (1,)), ((), ())), preferred_element_type=F32)


def _prenorm_matmul_body(x_ref, g_ref, w_ref, o_ref, hn_ref):
    @pl.when(pl.program_id(1) == 0)
    def _():
        x = x_ref[...]
        hn_ref[...] = (x * _rms_scale(x) * g_ref[...]).astype(BF16)

    o_ref[...] = _dot(hn_ref[...], w_ref[...]).astype(o_ref.dtype)


def prenorm_matmul(x, gain, w, *, tm, tn, out_dtype=BF16):
    m, d = x.shape
    n = w.shape[1]
    return pl.pallas_call(
        _prenorm_matmul_body,
        grid=(m // tm, n // tn),
        in_specs=[
            pl.BlockSpec((tm, d), lambda i, j: (i, 0)),
            pl.BlockSpec((1, d), lambda i, j: (0, 0)),
            pl.BlockSpec((d, tn), lambda i, j: (0, j)),
        ],
        out_specs=pl.BlockSpec((tm, tn), lambda i, j: (i, j)),
        out_shape=jax.ShapeDtypeStruct((m, n), out_dtype),
        scratch_shapes=[pltpu.VMEM((tm, d), BF16)],
        compiler_params=_cparams(("parallel", "arbitrary")),
        name="prenorm_matmul",
    )(x, gain.reshape(1, d), w)


def _dn_in_proj_body(x_ref, g_ref, w_ref, wab_ref, o_ref, ab_ref, hn_ref):
    @pl.when(pl.program_id(1) == 0)
    def _():
        x = x_ref[...]
        hn = (x * _rms_scale(x) * g_ref[...]).astype(BF16)
        hn_ref[...] = hn
        ab_ref[...] = _dot_nt(wab_ref[...], hn)

    o_ref[...] = _dot(hn_ref[...], w_ref[...]).astype(o_ref.dtype)


def dn_in_proj(x, gain, w_main, w_ab_t, *, tm, tn):
    m, d = x.shape
    n = w_main.shape[1]
    r = w_ab_t.shape[0]
    return pl.pallas_call(
        _dn_in_proj_body,
        grid=(m // tm, n // tn),
        in_specs=[
            pl.BlockSpec((tm, d), lambda i, j: (i, 0)),
            pl.BlockSpec((1, d), lambda i, j: (0, 0)),
            pl.BlockSpec((d, tn), lambda i, j: (0, j)),
            pl.BlockSpec((r, d), lambda i, j: (0, 0)),
        ],
        out_specs=[
            pl.BlockSpec((tm, tn), lambda i, j: (i, j)),
            pl.BlockSpec((r, tm), lambda i, j: (0, i)),
        ],
        out_shape=[
            jax.ShapeDtypeStruct((m, n), BF16),
            jax.ShapeDtypeStruct((r, m), F32),
        ],
        scratch_shapes=[pltpu.VMEM((tm, d), BF16)],
        compiler_params=_cparams(("parallel", "arbitrary")),
        name="dn_in_proj",
    )(x, gain.reshape(1, d), w_main, w_ab_t)


def _ffn_up_body(x_ref, g_ref, wg_ref, wu_ref, o_ref, hn_ref):
    @pl.when(pl.program_id(1) == 0)
    def _():
        x = x_ref[...]
        hn_ref[...] = (x * _rms_scale(x) * g_ref[...]).astype(BF16)

    hn = hn_ref[...]
    gate = _dot(hn, wg_ref[...])
    up = _dot(hn, wu_ref[...])
    o_ref[...] = (_silu(gate) * up).astype(o_ref.dtype)


def ffn_up(x, gain, wg, wu, *, tm, tn):
    m, d = x.shape
    n = wg.shape[1]
    return pl.pallas_call(
        _ffn_up_body,
        grid=(m // tm, n // tn),
        in_specs=[
            pl.BlockSpec((tm, d), lambda i, j: (i, 0)),
            pl.BlockSpec((1, d), lambda i, j: (0, 0)),
            pl.BlockSpec((d, tn), lambda i, j: (0, j)),
            pl.BlockSpec((d, tn), lambda i, j: (0, j)),
        ],
        out_specs=pl.BlockSpec((tm, tn), lambda i, j: (i, j)),
        out_shape=jax.ShapeDtypeStruct((m, n), BF16),
        scratch_shapes=[pltpu.VMEM((tm, d), BF16)],
        compiler_params=_cparams(("parallel", "arbitrary")),
        name="ffn_up",
    )(x, gain.reshape(1, d), wg, wu)


def _postnorm_residual(x, y, gain, scale):
    return x + scale * (y * _rms_scale(y) * gain)


def _down_body(a_ref, w_ref, x_ref, g_ref, o_ref, acc_ref, *, scale):
    k = pl.program_id(1)

    @pl.when(k == 0)
    def _():
        acc_ref[...] = jnp.zeros_like(acc_ref)

    acc_ref[...] += _dot(a_ref[...], w_ref[...])

    @pl.when(k == pl.num_programs(1) - 1)
    def _():
        o_ref[...] = _postnorm_residual(x_ref[...], acc_ref[...], g_ref[...], scale)


def ffn_down(act, w_out, x, gain, *, tm, tk, scale):
    m, kk = act.shape
    d = w_out.shape[1]
    return pl.pallas_call(
        functools.partial(_down_body, scale=scale),
        grid=(m // tm, kk // tk),
        in_specs=[
            pl.BlockSpec((tm, tk), lambda i, k: (i, k)),
            pl.BlockSpec((tk, d), lambda i, k: (k, 0)),
            pl.BlockSpec((tm, d), lambda i, k: (i, 0)),
            pl.BlockSpec((1, d), lambda i, k: (0, 0)),
        ],
        out_specs=pl.BlockSpec((tm, d), lambda i, k: (i, 0)),
        out_shape=jax.ShapeDtypeStruct((m, d), F32),
        scratch_shapes=[pltpu.VMEM((tm, d), F32)],
        compiler_params=_cparams(("parallel", "arbitrary")),
        name="ffn_down",
    )(act, w_out, x, gain.reshape(1, d))


def _dn_out_body(o_ref, om_ref, w1_ref, w2_ref, x_ref, g_ref, out_ref):
    y = _dot(o_ref[...], w1_ref[...]) + _dot(om_ref[...], w2_ref[...])
    out_ref[...] = _postnorm_residual(x_ref[...], y, g_ref[...], 1.0)


def dn_out_proj(o, o_mem, w1, w2, x, gain, *, tm):
    m, d = x.shape
    k1, k2 = o.shape[1], o_mem.shape[1]
    return pl.pallas_call(
        _dn_out_body,
        grid=(m // tm,),
        in_specs=[
            pl.BlockSpec((tm, k1), lambda i: (i, 0)),
            pl.BlockSpec((tm, k2), lambda i: (i, 0)),
            pl.BlockSpec((k1, d), lambda i: (0, 0)),
            pl.BlockSpec((k2, d), lambda i: (0, 0)),
            pl.BlockSpec((tm, d), lambda i: (i, 0)),
            pl.BlockSpec((1, d), lambda i: (0, 0)),
        ],
        out_specs=pl.BlockSpec((tm, d), lambda i: (i, 0)),
        out_shape=jax.ShapeDtypeStruct((m, d), F32),
        compiler_params=_cparams(("parallel",)),
        name="dn_out_proj",
    )(o, o_mem, w1, w2, x, gain.reshape(1, d))


def _dil_out_body(o0_ref, o1_ref, o2_ref, l0_ref, l1_ref, l2_ref, om_ref, w1_ref, w2_ref, x_ref, g_ref,
                  out_ref):
    l0, l1, l2 = l0_ref[...], l1_ref[...], l2_ref[...]
    mx = jnp.maximum(jnp.maximum(l0, l1), l2)
    e0, e1, e2 = jnp.exp(l0 - mx), jnp.exp(l1 - mx), jnp.exp(l2 - mx)
    o = (e0 * o0_ref[...].astype(F32) + e1 * o1_ref[...].astype(F32) + e2 * o2_ref[...].astype(F32))
    o = o / (e0 + e1 + e2)
    y = _dot(o.astype(BF16), w1_ref[...]) + _dot(om_ref[...], w2_ref[...])
    out_ref[...] = _postnorm_residual(x_ref[...], y, g_ref[...], 1.0)


def dil_out_proj(os, lses, o_mem, w1, w2, x, gain, *, tm):
    m, d = x.shape
    k1, k2 = os[0].shape[1], o_mem.shape[1]
    row = lambda kk: pl.BlockSpec((tm, kk), lambda i: (i, 0))
    return pl.pallas_call(
        _dil_out_body,
        grid=(m // tm,),
        in_specs=[row(k1)] * 3 + [row(k1)] * 3 + [
            row(k2),
            pl.BlockSpec((k1, d), lambda i: (0, 0)),
            pl.BlockSpec((k2, d), lambda i: (0, 0)),
            row(d),
            pl.BlockSpec((1, d), lambda i: (0, 0)),
        ],
        out_specs=row(d),
        out_shape=jax.ShapeDtypeStruct((m, d), F32),
        compiler_params=_cparams(("parallel",)),
        name="dil_out_proj",
    )(*os, *lses, o_mem, w1, w2, x, gain.reshape(1, d))


def _mem_attn_body(q_ref, k_ref, v_ref, o_ref):
    for h in range(MEM_HEADS):
        cs = slice(h * HEAD_DIM, (h + 1) * HEAD_DIM)
        s = _dot_nt(q_ref[:, cs], k_ref[:, cs]) * (HEAD_DIM ** -0.5)
        e = jnp.exp(s - jnp.max(s, axis=-1, keepdims=True))
        den = jnp.sum(e, axis=-1, keepdims=True)
        o_ref[:, cs] = (_dot(e.astype(BF16), v_ref[:, cs]) / den).astype(o_ref.dtype)


def mem_attention(proj, q_col_block, mem_kv, *, tm):
    m = proj.shape[0]
    mlen = mem_kv.shape[0]
    return pl.pallas_call(
        _mem_attn_body,
        grid=(m // tm,),
        in_specs=[
            pl.BlockSpec((tm, MEM_WIDTH), lambda i: (i, q_col_block)),
            pl.BlockSpec((mlen, MEM_WIDTH), lambda i: (0, 0)),
            pl.BlockSpec((mlen, MEM_WIDTH), lambda i: (0, 1)),
        ],
        out_specs=pl.BlockSpec((tm, MEM_WIDTH), lambda i: (i, 0)),
        out_shape=jax.ShapeDtypeStruct((m, MEM_WIDTH), BF16),
        compiler_params=_cparams(("parallel",)),
        name="mem_attention",
    )(proj, mem_kv, mem_kv)


def _gate_body(ab_ref, alog_ref, dtb_ref, gc_ref, beta_ref, gdk_ref):
    tl = ab_ref.shape[1]
    a = ab_ref[0:GATE_ROWS, :]
    b = ab_ref[GATE_ROWS:2 * GATE_ROWS, :]
    z = a + dtb_ref[...]
    softplus = jnp.maximum(z, 0.0) + jnp.log1p(jnp.exp(-jnp.abs(z)))
    g = -jnp.exp(alog_ref[...]) * softplus
    beta_ref[...] = jax.nn.sigmoid(b)
    r = lax.broadcasted_iota(jnp.int32, (V7X_LANES, V7X_LANES), 0)
    c = lax.broadcasted_iota(jnp.int32, (V7X_LANES, V7X_LANES), 1)
    same = _block_id(r, CHUNK) == _block_id(c, CHUNK)
    upper = jnp.where(same & (r <= c), 1.0, 0.0).astype(F32)
    ones = jnp.where(same, 1.0, 0.0).astype(F32)
    for s in range(tl // V7X_LANES):
        ls = slice(s * V7X_LANES, (s + 1) * V7X_LANES)
        gs = g[:, ls]
        gc = jnp.dot(gs, upper, preferred_element_type=F32, precision=lax.Precision.HIGHEST)
        gend = jnp.dot(gs, ones, preferred_element_type=F32, precision=lax.Precision.HIGHEST)
        gc_ref[:, ls] = gc
        gdk_ref[:, ls] = gend - gc


def dn_gates(ab_t, a_log, dt_bias, *, tl):
    r, s = ab_t.shape
    pad = lambda v: jnp.pad(v.astype(F32), (0, GATE_ROWS - v.shape[0])).reshape(GATE_ROWS, 1)
    out = jax.ShapeDtypeStruct((GATE_ROWS, s), F32)
    return pl.pallas_call(
        _gate_body,
        grid=(s // tl,),
        in_specs=[
            pl.BlockSpec((r, tl), lambda i: (0, i)),
            pl.BlockSpec((GATE_ROWS, 1), lambda i: (0, 0)),
            pl.BlockSpec((GATE_ROWS, 1), lambda i: (0, 0)),
        ],
        out_specs=[pl.BlockSpec((GATE_ROWS, tl), lambda i: (0, i))] * 3,
        out_shape=[out, out, out],
        compiler_params=_cparams(("parallel",)),
        name="dn_gates",
    )(ab_t, pad(a_log), pad(dt_bias))


def _col_bcast(row):
    n = row.shape[1]
    return jnp.broadcast_to(row, (n, n)).T


def _unit_lower_inverse(nmat, ri, ci):
    def quarter(b):
        same_block = _block_id(ri, 2 * b) == _block_id(ci, 2 * b)
        return same_block & ((ri & (2 * b - 1)) >= b) & ((ci & (2 * b - 1)) < b)

    eye = jnp.where(ri == ci, 1.0, 0.0).astype(F32)
    x = eye - jnp.where(quarter(1), nmat, 0.0)
    b = 2
    while b < CHUNK:
        ob = jnp.where(quarter(b), nmat, 0.0).astype(BF16)
        y = _dot(ob, x.astype(BF16))
        x = x - _dot(x.astype(BF16), y.astype(BF16))
        b *= 2
    return x


def _delta_body(q_ref, k_ref, v_ref, z_ref, qh_ref, kh_ref, vh_ref, wq_ref, wk_ref, wv_ref,
                gc_ref, beta_ref, gdk_ref, on_ref, o_ref, state_ref, buf_ref, *, tc):
    t = pl.program_id(1)
    halo = V7X_BF16_SUBLANES

    @pl.when(t == 0)
    def _():
        state_ref[...] = jnp.zeros_like(state_ref)

    def conv_silu(cur_ref, halo_ref, w_ref):
        prev = jnp.where(t == 0, 0.0, halo_ref[...].astype(F32))
        buf_ref[0:halo, :] = prev
        buf_ref[halo:halo + tc, :] = cur_ref[...].astype(F32)
        y = jnp.zeros((tc, HEAD_DIM), F32)
        for j in range(CONV_K):
            off = halo - (CONV_K - 1) + j
            y = y + w_ref[j:j + 1, :] * buf_ref[off:off + tc, :]
        return _silu(y)

    def l2n(x):
        return x * lax.rsqrt(jnp.sum(x * x, axis=-1, keepdims=True) + NORM_EPS)

    q_all = l2n(conv_silu(q_ref, qh_ref, wq_ref)) * (HEAD_DIM ** -0.5)
    k_all = l2n(conv_silu(k_ref, kh_ref, wk_ref))
    v_all = conv_silu(v_ref, vh_ref, wv_ref)

    ri = lax.broadcasted_iota(jnp.int32, (SUPER, SUPER), 0)
    ci = lax.broadcasted_iota(jnp.int32, (SUPER, SUPER), 1)
    same = _block_id(ri, CHUNK) == _block_id(ci, CHUNK)
    causal = same & (ri >= ci)
    strict = same & (ri > ci)

    state = state_ref[...]
    outs = []
    for s in range(tc // SUPER):
        rs = slice(s * SUPER, (s + 1) * SUPER)
        q, k, v = q_all[rs], k_all[rs], v_all[rs]
        g_row = gc_ref[:, rs]
        g_col = _col_bcast(g_row)
        beta_col = _col_bcast(beta_ref[:, rs])
        gdk_row = gdk_ref[:, rs]
        gdk_col = _col_bcast(gdk_row)
        decay = jnp.exp(jnp.where(causal, g_col - jnp.broadcast_to(g_row, (SUPER, SUPER)), -jnp.inf))
        exp_g = jnp.exp(g_col)
        g_last = jnp.broadcast_to(jnp.exp(g_row + gdk_row), (SUPER, SUPER))

        kb = k.astype(BF16)
        kk = _dot_nt(kb, kb)
        qk = _dot_nt(q.astype(BF16), kb)
        nmat = jnp.where(strict, beta_col * kk * decay, 0.0)
        tinv = _unit_lower_inverse(nmat, ri, ci)
        k_beta = k * beta_col
        rhs = jnp.concatenate([v * beta_col, k_beta * exp_g], axis=1).astype(BF16)
        sol = _dot(tinv.astype(BF16), rhs)
        u, w = sol[:, :HEAD_DIM], sol[:, HEAD_DIM:]
        attn = (qk * decay).astype(BF16)
        q_decay = q * exp_g
        kd_t = (k * jnp.exp(gdk_col)).T

        v_new = u
        for c in range(SUPER // CHUNK):
            cr = slice(c * CHUNK, (c + 1) * CHUNK)
            wq = jnp.concatenate([w[cr], q_decay[cr]], axis=0).astype(BF16)
            r = _dot(wq, state.astype(BF16))
            vn_c = u[cr] - r[:CHUNK]
            if c == 0:
                v_new = jnp.concatenate([vn_c, jnp.zeros_like(vn_c)], axis=0)
            else:
                v_new = jnp.concatenate([v_new[:CHUNK], vn_c], axis=0)
            vb = v_new.astype(BF16)
            outs.append(r[CHUNK:] + _dot(attn[cr], vb))
            in_chunk = _block_id(ci, CHUNK) == c
            kd_c = jnp.where(in_chunk, kd_t, 0.0).astype(BF16)
            gl = jnp.broadcast_to(g_last[:, c * CHUNK:c * CHUNK + 1], (HEAD_DIM, HEAD_DIM))
            state = state * gl + _dot(kd_c, vb)

    state_ref[...] = state
    o = jnp.concatenate(outs, axis=0)
    o = o * _rms_scale(o) * on_ref[...] * _silu(z_ref[...].astype(F32))
    o_ref[...] = o.astype(o_ref.dtype)


def delta_core(proj, conv_w, gc, beta, gdk, o_norm, *, tc):
    s = proj.shape[0]
    nh = DN_HEADS
    halo = V7X_BF16_SUBLANES
    hb = tc // halo

    def cur(off):
        return pl.BlockSpec((tc, HEAD_DIM), lambda h, t: (t, off + h))

    def prev(off):
        return pl.BlockSpec((halo, HEAD_DIM), lambda h, t: (jnp.maximum(t * hb - 1, 0), off + h))

    def cw(off):
        return pl.BlockSpec((CONV_K, HEAD_DIM), lambda h, t: (0, off + h))

    gate = pl.BlockSpec((None, 1, tc), lambda h, t: (h, 0, t))
    g3 = lambda a: a.reshape(GATE_ROWS, 1, s)
    return pl.pallas_call(
        functools.partial(_delta_body, tc=tc),
        grid=(nh, s // tc),
        in_specs=[cur(0), cur(nh), cur(2 * nh), cur(3 * nh), prev(0), prev(nh), prev(2 * nh),
                  cw(0), cw(nh), cw(2 * nh), gate, gate, gate,
                  pl.BlockSpec((1, HEAD_DIM), lambda h, t: (0, 0))],
        out_specs=pl.BlockSpec((tc, HEAD_DIM), lambda h, t: (t, h)),
        out_shape=jax.ShapeDtypeStruct((s, nh * HEAD_DIM), BF16),
        scratch_shapes=[pltpu.VMEM((HEAD_DIM, HEAD_DIM), F32),
                        pltpu.VMEM((tc + halo, HEAD_DIM), F32)],
        compiler_params=_cparams(("parallel", "arbitrary")),
        name="delta_core",
    )(proj, proj, proj, proj, proj, proj, proj, conv_w, conv_w, conv_w,
      g3(gc), g3(beta), g3(gdk), o_norm.reshape(1, HEAD_DIM))


def _dil_attn_body(q_ref, kc_ref, kp_ref, vc_ref, vp_ref, o_ref, l_ref, *, dil, slopes, qb):
    n = pl.program_id(1)
    blk = DIL_BLOCK
    ri = lax.broadcasted_iota(jnp.int32, (blk, 2 * blk), 0)
    ci = lax.broadcasted_iota(jnp.int32, (blk, 2 * blk), 1)
    rel = ri + blk - ci
    window = (rel >= 0) & (rel <= blk)
    dist = (rel * dil).astype(F32)
    for b in range(qb):
        rows = slice(b * blk, (b + 1) * blk)
        if b == 0:
            valid = window & ((ci >= blk) | (n > 0))
        else:
            valid = window
        for h in range(DIL_HEADS_PER_GROUP):
            cs = slice(h * HEAD_DIM, (h + 1) * HEAD_DIM)
            if b == 0:
                kp, vp = kp_ref[:, cs], vp_ref[:, cs]
            else:
                prows = slice((b - 1) * blk, b * blk)
                kp, vp = kc_ref[prows, cs], vc_ref[prows, cs]
            keys = jnp.concatenate([kp, kc_ref[rows, cs]], axis=0)
            vals = jnp.concatenate([vp, vc_ref[rows, cs]], axis=0)
            s = _dot_nt(q_ref[rows, cs], keys) * (HEAD_DIM ** -0.5)
            s = jnp.where(valid, s - slopes[h] * dist, -jnp.inf)
            mx = jnp.max(s, axis=-1, keepdims=True)
            e = jnp.exp(s - mx)
            den = jnp.sum(e, axis=-1, keepdims=True)
            o = _dot(e.astype(BF16), vals) / den
            o_ref[rows, cs] = o.astype(o_ref.dtype)
            l_ref[rows, cs] = jnp.broadcast_to(mx + jnp.log(den), (blk, HEAD_DIM))


def dilated_group_attention(qproj, kv, group, *, qb):
    win, dil = DIL_GROUPS[group]
    assert win // dil == DIL_BLOCK
    s, qn = qproj.shape
    kn = kv.shape[1]
    length = s // dil
    gw = DIL_HEADS_PER_GROUP * HEAD_DIM
    nb = length // DIL_BLOCK
    qb = min(qb, nb)
    rows = qb * DIL_BLOCK
    heads = range(group * DIL_HEADS_PER_GROUP, (group + 1) * DIL_HEADS_PER_GROUP)
    slopes = tuple(2.0 ** (-8.0 * (h + 1) / DIL_HEADS) for h in heads)
    qv = qproj.reshape(length, dil * qn)
    kvv = kv.reshape(length, dil * kn)
    qcb, kcb, vcb = qn // gw, kn // gw, DIL_WIDTH // gw

    def cur(per_stream, off):
        return pl.BlockSpec((rows, gw), lambda c, n: (n, c * per_stream + off))

    def prev(per_stream, off):
        return pl.BlockSpec((DIL_BLOCK, gw), lambda c, n: (jnp.maximum(n * qb - 1, 0), c * per_stream + off))

    out_spec = pl.BlockSpec((rows, gw), lambda c, n: (n, c))
    o, lse = pl.pallas_call(
        functools.partial(_dil_attn_body, dil=dil, slopes=slopes, qb=qb),
        grid=(dil, nb // qb),
        in_specs=[cur(qcb, group), cur(kcb, group), prev(kcb, group),
                  cur(kcb, vcb + group), prev(kcb, vcb + group)],
        out_specs=[out_spec, out_spec],
        out_shape=[jax.ShapeDtypeStruct((length, dil * gw), BF16),
                   jax.ShapeDtypeStruct((length, dil * gw), F32)],
        compiler_params=_cparams(("parallel", "arbitrary")),
        name=f"dilated_attention_g{group}",
    )(qv, kvv, kvv, kvv, kvv)
    return o.reshape(s, gw), lse.reshape(s, gw)


def _ffn(x, g_pre, g_post, w_in, w_out):
    pad_c = lambda w: jnp.pad(w, ((0, 0), (0, D_FF_PAD - D_FF))).astype(BF16)
    wg, wu = pad_c(w_in[:, :D_FF]), pad_c(w_in[:, D_FF:])
    wo = jnp.pad(w_out, ((0, D_FF_PAD - D_FF), (0, 0))).astype(BF16)
    act = ffn_up(x, g_pre, wg, wu, tm=1024, tn=FF_TILE)
    return ffn_down(act, wo, x, g_post, tm=512, tk=FF_TILE, scale=0.5)


def _memory_kv(mem, gain, w):
    return prenorm_matmul(mem, gain, w.astype(BF16), tm=mem.shape[0], tn=512)


def _deltanet_layer(x, g_pre, g_post, mem_kv, w_in, conv_w, a_log, dt_bias, o_norm, w_out):
    qkvz = 4 * DN_WIDTH
    w_main = jnp.concatenate([w_in[:, :qkvz], w_in[:, qkvz + 2 * DN_HEADS:]], axis=1).astype(BF16)
    gate_w = lambda lo: jnp.pad(w_in[:, lo:lo + DN_HEADS].T, ((0, GATE_ROWS - DN_HEADS), (0, 0)))
    w_ab_t = jnp.concatenate([gate_w(qkvz), gate_w(qkvz + DN_HEADS)], axis=0).astype(BF16)
    proj, ab_t = dn_in_proj(x, g_pre, w_main, w_ab_t, tm=1024, tn=512)
    gc, beta, gdk = dn_gates(ab_t, a_log, dt_bias, tl=1024)
    o = delta_core(proj, conv_w.astype(F32), gc, beta, gdk, o_norm, tc=256)
    o_mem = mem_attention(proj, qkvz // MEM_WIDTH, mem_kv, tm=512)
    w_out = w_out.astype(BF16)
    return dn_out_proj(o, o_mem, w_out[:DN_WIDTH], w_out[DN_WIDTH:], x, g_post, tm=512)


def _dilated_layer(x, g_pre, g_post, kv, mem_kv, w_in, w_out):
    proj = prenorm_matmul(x, g_pre, w_in.astype(BF16), tm=1024, tn=512)
    os, lses = [], []
    for gi in range(len(DIL_GROUPS)):
        o, lse = dilated_group_attention(proj, kv, gi, qb=4)
        os.append(o)
        lses.append(lse)
    o_mem = mem_attention(proj, DIL_WIDTH // MEM_WIDTH, mem_kv, tm=512)
    w_out = w_out.astype(BF16)
    return dil_out_proj(os, lses, o_mem, w_out[:DIL_OUT], w_out[DIL_OUT:], x, g_post, tm=512)


def kernel(x, mem, norm_gains, ffn_w_in, ffn_w_out, mem_norm_gain, w_mem_kv, dn_w_in, dn_conv, dn_a_log,
           dn_dt_bias, dn_o_norm, dn_w_out, kv_norm_gain, w_kv, dil_w_in, dil_w_out):
    batch, seq, d = x.shape
    assert batch == 1
    depth = norm_gains.shape[0]
    n_a = dn_w_in.shape[0]
    xs = x.reshape(seq, d)
    mem2 = mem.reshape(mem.shape[1], d)
    kv = None
    for l in range(depth):
        gains = norm_gains[l]
        if l == n_a:
            kv = prenorm_matmul(xs, kv_norm_gain, w_kv.astype(BF16), tm=1024, tn=512)
        xs = _ffn(xs, gains[0], gains[1], ffn_w_in[l, 0], ffn_w_out[l, 0])
        mem_kv = _memory_kv(mem2, mem_norm_gain[l], w_mem_kv[l])
        if l < n_a:
            xs = _deltanet_layer(xs, gains[2], gains[3], mem_kv, dn_w_in[l], dn_conv[l], dn_a_log[l],
                                 dn_dt_bias[l], dn_o_norm[l], dn_w_out[l])
        else:
            i = l - n_a
            xs = _dilated_layer(xs, gains[2], gains[3], kv, mem_kv, dil_w_in[i], dil_w_out[i])
        xs = _ffn(xs, gains[4], gains[5], ffn_w_in[l, 1], ffn_w_out[l, 1])
    return xs.reshape(batch, seq, d)
```

```python
import functools
import math

import jax
import jax.numpy as jnp
from jax import lax
from jax.experimental import pallas as pl
from jax.experimental.pallas import tpu as pltpu

F32 = jnp.float32
BF16 = jnp.bfloat16

D_MODEL = 2048
HEAD_DIM = 128
DN_HEADS = 12
DN_WIDTH = DN_HEADS * HEAD_DIM
CONV_K = 4
CHUNK = 64
MEM_HEADS = 4
MEM_WIDTH = MEM_HEADS * HEAD_DIM
DIL_GROUPS = ((128, 1), (512, 4), (2048, 16))
DIL_HEADS_PER_GROUP = 4
DIL_HEADS = DIL_HEADS_PER_GROUP * len(DIL_GROUPS)
DIL_WIDTH = DIL_HEADS * HEAD_DIM
DIL_OUT = DIL_HEADS_PER_GROUP * HEAD_DIM
DIL_BLOCK = 128
D_FF = ((8 * D_MODEL // 3 + 127) // 128) * 128
NORM_EPS = 1e-6

V7X_LANES = 128
V7X_BF16_SUBLANES = 16
V7X_VMEM_LIMIT_BYTES = 56 * 1024 * 1024

FF_TILE = 512
D_FF_PAD = -(-D_FF // FF_TILE) * FF_TILE
GATE_ROWS = 16
SUPER = 2 * CHUNK


def _cparams(sem):
    return pltpu.CompilerParams(dimension_semantics=sem, vmem_limit_bytes=V7X_VMEM_LIMIT_BYTES)


def _rms_scale(x):
    return lax.rsqrt(jnp.mean(x * x, axis=-1, keepdims=True) + NORM_EPS)


def _silu(x):
    return x * jax.nn.sigmoid(x)


def _block_id(idx, size):
    assert size & (size - 1) == 0
    return lax.shift_right_logical(idx, size.bit_length() - 1)


def _dot(a, b):
    return jnp.dot(a, b, preferred_element_type=F32)


def _dot_nt(a, b):
    return lax.dot_general(a, b, (((1,), (1,)), ((), ())), preferred_element_type=F32)


def _prenorm_matmul_body(x_ref, g_ref, w_ref, o_ref, hn_ref):
    @pl.when(pl.program_id(1) == 0)
    def _():
        x = x_ref[...]
        hn_ref[...] = (x * _rms_scale(x) * g_ref[...]).astype(BF16)

    o_ref[...] = _dot(hn_ref[...], w_ref[...]).astype(o_ref.dtype)


def prenorm_matmul(x, gain, w, *, tm, tn, out_dtype=BF16):
    m, d = x.shape
    n = w.shape[1]
    return pl.pallas_call(
        _prenorm_matmul_body,
        grid=(m // tm, n // tn),
        in_specs=[
            pl.BlockSpec((tm, d), lambda i, j: (i, 0)),
            pl.BlockSpec((1, d), lambda i, j: (0, 0)),
            pl.BlockSpec((d, tn), lambda i, j: (0, j)),
        ],
        out_specs=pl.BlockSpec((tm, tn), lambda i, j: (i, j)),
        out_shape=jax.ShapeDtypeStruct((m, n), out_dtype),
        scratch_shapes=[pltpu.VMEM((tm, d), BF16)],
        compiler_params=_cparams(("parallel", "arbitrary")),
        name="prenorm_matmul",
    )(x, gain.reshape(1, d), w)


def _dn_in_proj_body(x_ref, g_ref, w_ref, wab_ref, o_ref, ab_ref, hn_ref):
    @pl.when(pl.program_id(1) == 0)
    def _():
        x = x_ref[...]
        hn = (x * _rms_scale(x) * g_ref[...]).astype(BF16)
        hn_ref[...] = hn
        ab_ref[...] = _dot_nt(wab_ref[...], hn)

    o_ref[...] = _dot(hn_ref[...], w_ref[...]).astype(o_ref.dtype)


def dn_in_proj(x, gain, w_main, w_ab_t, *, tm, tn):
    m, d = x.shape
    n = w_main.shape[1]
    r = w_ab_t.shape[0]
    return pl.pallas_call(
        _dn_in_proj_body,
        grid=(m // tm, n // tn),
        in_specs=[
            pl.BlockSpec((tm, d), lambda i, j: (i, 0)),
            pl.BlockSpec((1, d), lambda i, j: (0, 0)),
            pl.BlockSpec((d, tn), lambda i, j: (0, j)),
            pl.BlockSpec((r, d), lambda i, j: (0, 0)),
        ],
        out_specs=[
            pl.BlockSpec((tm, tn), lambda i, j: (i, j)),
            pl.BlockSpec((r, tm), lambda i, j: (0, i)),
        ],
        out_shape=[
            jax.ShapeDtypeStruct((m, n), BF16),
            jax.ShapeDtypeStruct((r, m), F32),
        ],
        scratch_shapes=[pltpu.VMEM((tm, d), BF16)],
        compiler_params=_cparams(("parallel", "arbitrary")),
        name="dn_in_proj",
    )(x, gain.reshape(1, d), w_main, w_ab_t)


def _postnorm_residual(x, y, gain, scale):
    return x + scale * (y * _rms_scale(y) * gain)


def _ffn_body(x_ref, gpre_ref, wg_ref, wu_ref, wo_ref, gpost_ref, o_ref, hn_ref):
    j = pl.program_id(1)

    @pl.when(j == 0)
    def _():
        x = x_ref[...]
        hn_ref[...] = (x * _rms_scale(x) * gpre_ref[...]).astype(BF16)
        o_ref[...] = jnp.zeros_like(o_ref)

    hn = hn_ref[...]
    act = (_silu(_dot(hn, wg_ref[...])) * _dot(hn, wu_ref[...])).astype(BF16)
    o_ref[...] += _dot(act, wo_ref[...])

    @pl.when(j == pl.num_programs(1) - 1)
    def _():
        o_ref[...] = _postnorm_residual(x_ref[...], o_ref[...], gpost_ref[...], 0.5)


def ffn(x, g_pre, g_post, w_gu, w_o, *, tm, tf):
    m, d = x.shape
    ffp = w_o.shape[0]
    nf = ffp // tf
    return pl.pallas_call(
        _ffn_body,
        grid=(m // tm, nf),
        in_specs=[
            pl.BlockSpec((tm, d), lambda i, j: (i, 0)),
            pl.BlockSpec((1, d), lambda i, j: (0, 0)),
            pl.BlockSpec((d, tf), lambda i, j: (0, j)),
            pl.BlockSpec((d, tf), lambda i, j: (0, nf + j)),
            pl.BlockSpec((tf, d), lambda i, j: (j, 0)),
            pl.BlockSpec((1, d), lambda i, j: (0, 0)),
        ],
        out_specs=pl.BlockSpec((tm, d), lambda i, j: (i, 0)),
        out_shape=jax.ShapeDtypeStruct((m, d), F32),
        scratch_shapes=[pltpu.VMEM((tm, d), BF16)],
        compiler_params=_cparams(("parallel", "arbitrary")),
        name="ffn",
    )(x, g_pre.reshape(1, d), w_gu, w_gu, w_o, g_post.reshape(1, d))


def _dn_out_body(o_ref, om_ref, w1_ref, w2_ref, x_ref, g_ref, out_ref):
    y = _dot(o_ref[...], w1_ref[...]) + _dot(om_ref[...], w2_ref[...])
    out_ref[...] = _postnorm_residual(x_ref[...], y, g_ref[...], 1.0)


def dn_out_proj(o, o_mem, w1, w2, x, gain, *, tm):
    m, d = x.shape
    k1, k2 = o.shape[1], o_mem.shape[1]
    return pl.pallas_call(
        _dn_out_body,
        grid=(m // tm,),
        in_specs=[
            pl.BlockSpec((tm, k1), lambda i: (i, 0)),
            pl.BlockSpec((tm, k2), lambda i: (i, 0)),
            pl.BlockSpec((k1, d), lambda i: (0, 0)),
            pl.BlockSpec((k2, d), lambda i: (0, 0)),
            pl.BlockSpec((tm, d), lambda i: (i, 0)),
            pl.BlockSpec((1, d), lambda i: (0, 0)),
        ],
        out_specs=pl.BlockSpec((tm, d), lambda i: (i, 0)),
        out_shape=jax.ShapeDtypeStruct((m, d), F32),
        compiler_params=_cparams(("parallel",)),
        name="dn_out_proj",
    )(o, o_mem, w1, w2, x, gain.reshape(1, d))


def _dil_out_body(o0_ref, o1_ref, o2_ref, l0_ref, l1_ref, l2_ref, om_ref, w1_ref, w2_ref, x_ref, g_ref,
                  out_ref):
    l0, l1, l2 = l0_ref[...], l1_ref[...], l2_ref[...]
    mx = jnp.maximum(jnp.maximum(l0, l1), l2)
    e0, e1, e2 = jnp.exp(l0 - mx), jnp.exp(l1 - mx), jnp.exp(l2 - mx)
    o = (e0 * o0_ref[...].astype(F32) + e1 * o1_ref[...].astype(F32) + e2 * o2_ref[...].astype(F32))
    o = o / (e0 + e1 + e2)
    y = _dot(o.astype(BF16), w1_ref[...]) + _dot(om_ref[...], w2_ref[...])
    out_ref[...] = _postnorm_residual(x_ref[...], y, g_ref[...], 1.0)


def dil_out_proj(os, lses, o_mem, w1, w2, x, gain, *, tm):
    m, d = x.shape
    k1, k2 = os[0].shape[1], o_mem.shape[1]
    row = lambda kk: pl.BlockSpec((tm, kk), lambda i: (i, 0))
    return pl.pallas_call(
        _dil_out_body,
        grid=(m // tm,),
        in_specs=[row(k1)] * 3 + [row(k1)] * 3 + [
            row(k2),
            pl.BlockSpec((k1, d), lambda i: (0, 0)),
            pl.BlockSpec((k2, d), lambda i: (0, 0)),
            row(d),
            pl.BlockSpec((1, d), lambda i: (0, 0)),
        ],
        out_specs=row(d),
        out_shape=jax.ShapeDtypeStruct((m, d), F32),
        compiler_params=_cparams(("parallel",)),
        name="dil_out_proj",
    )(*os, *lses, o_mem, w1, w2, x, gain.reshape(1, d))


def _mem_attn_body(q_ref, k_ref, v_ref, o_ref):
    for h in range(MEM_HEADS):
        cs = slice(h * HEAD_DIM, (h + 1) * HEAD_DIM)
        s = _dot_nt(q_ref[:, cs], k_ref[:, cs]) * (HEAD_DIM ** -0.5)
        e = jnp.exp(s - jnp.max(s, axis=-1, keepdims=True))
        den = jnp.sum(e, axis=-1, keepdims=True)
        o_ref[:, cs] = (_dot(e.astype(BF16), v_ref[:, cs]) / den).astype(o_ref.dtype)


def mem_attention(proj, q_col_block, mem_kv, *, tm):
    m = proj.shape[0]
    mlen = mem_kv.shape[0]
    return pl.pallas_call(
        _mem_attn_body,
        grid=(m // tm,),
        in_specs=[
            pl.BlockSpec((tm, MEM_WIDTH), lambda i: (i, q_col_block)),
            pl.BlockSpec((mlen, MEM_WIDTH), lambda i: (0, 0)),
            pl.BlockSpec((mlen, MEM_WIDTH), lambda i: (0, 1)),
        ],
        out_specs=pl.BlockSpec((tm, MEM_WIDTH), lambda i: (i, 0)),
        out_shape=jax.ShapeDtypeStruct((m, MEM_WIDTH), BF16),
        compiler_params=_cparams(("parallel",)),
        name="mem_attention",
    )(proj, mem_kv, mem_kv)


def _gate_body(ab_ref, alog_ref, dtb_ref, gc_ref, beta_ref, gdk_ref):
    tl = ab_ref.shape[1]
    a = ab_ref[0:GATE_ROWS, :]
    b = ab_ref[GATE_ROWS:2 * GATE_ROWS, :]
    z = a + dtb_ref[...]
    softplus = jnp.maximum(z, 0.0) + jnp.log1p(jnp.exp(-jnp.abs(z)))
    g = -jnp.exp(alog_ref[...]) * softplus
    beta_ref[...] = jax.nn.sigmoid(b)
    r = lax.broadcasted_iota(jnp.int32, (V7X_LANES, V7X_LANES), 0)
    c = lax.broadcasted_iota(jnp.int32, (V7X_LANES, V7X_LANES), 1)
    same = _block_id(r, CHUNK) == _block_id(c, CHUNK)
    upper = jnp.where(same & (r <= c), 1.0, 0.0).astype(F32)
    ones = jnp.where(same, 1.0, 0.0).astype(F32)
    for s in range(tl // V7X_LANES):
        ls = slice(s * V7X_LANES, (s + 1) * V7X_LANES)
        gs = g[:, ls]
        gc = jnp.dot(gs, upper, preferred_element_type=F32, precision=lax.Precision.HIGHEST)
        gend = jnp.dot(gs, ones, preferred_element_type=F32, precision=lax.Precision.HIGHEST)
        gc_ref[:, ls] = gc
        gdk_ref[:, ls] = gend - gc


def dn_gates(ab_t, a_log, dt_bias, *, tl):
    r, s = ab_t.shape
    pad = lambda v: jnp.pad(v.astype(F32), (0, GATE_ROWS - v.shape[0])).reshape(GATE_ROWS, 1)
    out = jax.ShapeDtypeStruct((GATE_ROWS, s), F32)
    return pl.pallas_call(
        _gate_body,
        grid=(s // tl,),
        in_specs=[
            pl.BlockSpec((r, tl), lambda i: (0, i)),
            pl.BlockSpec((GATE_ROWS, 1), lambda i: (0, 0)),
            pl.BlockSpec((GATE_ROWS, 1), lambda i: (0, 0)),
        ],
        out_specs=[pl.BlockSpec((GATE_ROWS, tl), lambda i: (0, i))] * 3,
        out_shape=[out, out, out],
        compiler_params=_cparams(("parallel",)),
        name="dn_gates",
    )(ab_t, pad(a_log), pad(dt_bias))


def _col_bcast(row):
    n = row.shape[1]
    return jnp.broadcast_to(row, (n, n)).T


def _unit_lower_inverses(nmats, ri, ci):
    def quarter(b):
        same_block = _block_id(ri, 2 * b) == _block_id(ci, 2 * b)
        return same_block & ((ri & (2 * b - 1)) >= b) & ((ci & (2 * b - 1)) < b)

    eye = jnp.where(ri == ci, 1.0, 0.0).astype(F32)
    first = quarter(1)
    xs = [eye - jnp.where(first, n, 0.0) for n in nmats]
    b = 2
    while b < CHUNK:
        mask = quarter(b)
        ys = [_dot(jnp.where(mask, n, 0.0).astype(BF16), x.astype(BF16)) for n, x in zip(nmats, xs)]
        xs = [x - _dot(x.astype(BF16), y.astype(BF16)) for x, y in zip(xs, ys)]
        b *= 2
    return xs


def _delta_body(q_ref, k_ref, v_ref, z_ref, qh_ref, kh_ref, vh_ref, wq_ref, wk_ref, wv_ref,
                gc_ref, beta_ref, gdk_ref, on_ref, o_ref, state_ref, buf_ref, *, tc, hb):
    t = pl.program_id(1)
    halo = V7X_BF16_SUBLANES
    n_super = tc // SUPER
    n_chunk = SUPER // CHUNK

    @pl.when(t == 0)
    def _():
        state_ref[...] = jnp.zeros_like(state_ref)

    states = [state_ref[h] for h in range(hb)]

    ri = lax.broadcasted_iota(jnp.int32, (SUPER, SUPER), 0)
    ci = lax.broadcasted_iota(jnp.int32, (SUPER, SUPER), 1)
    same = _block_id(ri, CHUNK) == _block_id(ci, CHUNK)
    causal = same & (ri >= ci)
    strict = same & (ri > ci)

    def conv_silu(slot, cur_ref, halo_ref, w_ref, cs):
        buf_ref[slot, 0:halo, :] = jnp.where(t == 0, 0.0, halo_ref[:, cs].astype(F32))
        buf_ref[slot, halo:halo + tc, :] = cur_ref[:, cs].astype(F32)
        y = jnp.zeros((tc, HEAD_DIM), F32)
        for j in range(CONV_K):
            off = halo - (CONV_K - 1) + j
            y = y + w_ref[j:j + 1, cs] * buf_ref[slot, off:off + tc, :]
        return _silu(y)

    def l2n(x):
        return x * lax.rsqrt(jnp.sum(x * x, axis=-1, keepdims=True) + NORM_EPS)

    heads = range(hb)
    lanes = [slice(h * HEAD_DIM, (h + 1) * HEAD_DIM) for h in heads]
    q_all = [l2n(conv_silu(3 * h, q_ref, qh_ref, wq_ref, lanes[h])) * (HEAD_DIM ** -0.5) for h in heads]
    k_all = [l2n(conv_silu(3 * h + 1, k_ref, kh_ref, wk_ref, lanes[h])) for h in heads]
    v_all = [conv_silu(3 * h + 2, v_ref, vh_ref, wv_ref, lanes[h]) for h in heads]

    units = [(h, s) for s in range(n_super) for h in heads]
    rows = lambda s: slice(s * SUPER, (s + 1) * SUPER)
    q = [q_all[h][rows(s)] for h, s in units]
    k = [k_all[h][rows(s)] for h, s in units]
    v = [v_all[h][rows(s)] for h, s in units]
    g_row = [gc_ref[h, :, rows(s)] for h, s in units]
    gdk_row = [gdk_ref[h, :, rows(s)] for h, s in units]
    g_col = [_col_bcast(g) for g in g_row]
    gdk_col = [_col_bcast(g) for g in gdk_row]
    beta_col = [_col_bcast(beta_ref[h, :, rows(s)]) for h, s in units]
    decay = [jnp.exp(jnp.where(causal, gc - jnp.broadcast_to(gr, (SUPER, SUPER)), -jnp.inf))
             for gc, gr in zip(g_col, g_row)]
    exp_g = [jnp.exp(g) for g in g_col]
    g_last = [jnp.broadcast_to(jnp.exp(gr + gd), (SUPER, SUPER)) for gr, gd in zip(g_row, gdk_row)]

    kb = [x.astype(BF16) for x in k]
    kk = [_dot_nt(x, x) for x in kb]
    qk = [_dot_nt(a.astype(BF16), b) for a, b in zip(q, kb)]
    nmat = [jnp.where(strict, b * x * d, 0.0) for b, x, d in zip(beta_col, kk, decay)]
    tinv = _unit_lower_inverses(nmat, ri, ci)
    rhs = [jnp.concatenate([kx * b * e, vx * b], axis=1).astype(BF16)
           for kx, vx, b, e in zip(k, v, beta_col, exp_g)]
    wu = [_dot(x.astype(BF16), r).astype(BF16) for x, r in zip(tinv, rhs)]
    attn = [(a * d).astype(BF16) for a, d in zip(qk, decay)]
    a_wu = [_dot(a, x) for a, x in zip(attn, wu)]
    q_eff = [(qx * e - aw[:, :HEAD_DIM]).astype(BF16) for qx, e, aw in zip(q, exp_g, a_wu)]
    o_intra = [aw[:, HEAD_DIM:] for aw in a_wu]
    kd_t = [(kx * jnp.exp(g)).T for kx, g in zip(k, gdk_col)]
    in_chunk = [_block_id(ci, CHUNK) == c for c in range(n_chunk)]
    pr = [[_dot(jnp.where(in_chunk[c], kt, 0.0).astype(BF16), x) for c in range(n_chunk)]
          for kt, x in zip(kd_t, wu)]

    outs = [[] for _ in heads]
    for s in range(n_super):
        for c in range(n_chunk):
            cr = slice(c * CHUNK, (c + 1) * CHUNK)
            for h in heads:
                u = s * hb + h
                lhs = jnp.concatenate([q_eff[u][cr], pr[u][c][:, :HEAD_DIM].astype(BF16)], axis=0)
                r = _dot(lhs, states[h].astype(BF16))
                outs[h].append(r[:CHUNK] + o_intra[u][cr])
                gl = jnp.broadcast_to(g_last[u][:, c * CHUNK:c * CHUNK + 1], (HEAD_DIM, HEAD_DIM))
                states[h] = states[h] * gl + (pr[u][c][:, HEAD_DIM:] - r[CHUNK:])

    o_norm = on_ref[...]
    for h in heads:
        o = jnp.concatenate(outs[h], axis=0)
        o = o * _rms_scale(o) * o_norm * _silu(z_ref[:, lanes[h]].astype(F32))
        o_ref[:, lanes[h]] = o.astype(o_ref.dtype)
    for h in heads:
        state_ref[h] = states[h]


def delta_core(proj, conv_w, gc, beta, gdk, o_norm, *, tc, hb):
    s = proj.shape[0]
    nh = DN_HEADS
    assert nh % hb == 0
    ng = nh // hb
    halo = V7X_BF16_SUBLANES
    nhalo = tc // halo
    width = hb * HEAD_DIM

    def cur(sec):
        return pl.BlockSpec((tc, width), lambda g, t: (t, sec * ng + g))

    def prev(sec):
        return pl.BlockSpec((halo, width), lambda g, t: (jnp.maximum(t * nhalo - 1, 0), sec * ng + g))

    def cw(sec):
        return pl.BlockSpec((CONV_K, width), lambda g, t: (0, sec * ng + g))

    gate = pl.BlockSpec((hb, 1, tc), lambda g, t: (g, 0, t))
    g3 = lambda a: a.reshape(GATE_ROWS, 1, s)
    return pl.pallas_call(
        functools.partial(_delta_body, tc=tc, hb=hb),
        grid=(ng, s // tc),
        in_specs=[cur(0), cur(1), cur(2), cur(3), prev(0), prev(1), prev(2),
                  cw(0), cw(1), cw(2), gate, gate, gate,
                  pl.BlockSpec((1, HEAD_DIM), lambda g, t: (0, 0))],
        out_specs=pl.BlockSpec((tc, width), lambda g, t: (t, g)),
        out_shape=jax.ShapeDtypeStruct((s, nh * HEAD_DIM), BF16),
        scratch_shapes=[pltpu.VMEM((hb, HEAD_DIM, HEAD_DIM), F32),
                        pltpu.VMEM((3 * hb, tc + halo, HEAD_DIM), F32)],
        compiler_params=_cparams(("parallel", "arbitrary")),
        name="delta_core",
    )(proj, proj, proj, proj, proj, proj, proj, conv_w, conv_w, conv_w,
      g3(gc), g3(beta), g3(gdk), o_norm.reshape(1, HEAD_DIM))


def _dil_attn_body(q_ref, kc_ref, kp_ref, vc_ref, vp_ref, o_ref, l_ref, *, dil, slopes, qb):
    n = pl.program_id(1)
    blk = DIL_BLOCK
    ri = lax.broadcasted_iota(jnp.int32, (blk, 2 * blk), 0)
    ci = lax.broadcasted_iota(jnp.int32, (blk, 2 * blk), 1)
    rel = ri + blk - ci
    window = (rel >= 0) & (rel <= blk)
    dist = (rel * dil).astype(F32)
    for b in range(qb):
        rows = slice(b * blk, (b + 1) * blk)
        if b == 0:
            valid = window & ((ci >= blk) | (n > 0))
        else:
            valid = window
        for h in range(DIL_HEADS_PER_GROUP):
            cs = slice(h * HEAD_DIM, (h + 1) * HEAD_DIM)
            if b == 0:
                kp, vp = kp_ref[:, cs], vp_ref[:, cs]
            else:
                prows = slice((b - 1) * blk, b * blk)
                kp, vp = kc_ref[prows, cs], vc_ref[prows, cs]
            keys = jnp.concatenate([kp, kc_ref[rows, cs]], axis=0)
            vals = jnp.concatenate([vp, vc_ref[rows, cs]], axis=0)
            s = _dot_nt(q_ref[rows, cs], keys) * (HEAD_DIM ** -0.5)
            s = jnp.where(valid, s - slopes[h] * dist, -jnp.inf)
            mx = jnp.max(s, axis=-1, keepdims=True)
            e = jnp.exp(s - mx)
            den = jnp.sum(e, axis=-1, keepdims=True)
            o = _dot(e.astype(BF16), vals) / den
            o_ref[rows, cs] = o.astype(o_ref.dtype)
            l_ref[rows, cs] = jnp.broadcast_to(mx + jnp.log(den), (blk, HEAD_DIM))


def dilated_group_attention(qproj, kv, group, *, qb):
    win, dil = DIL_GROUPS[group]
    assert win // dil == DIL_BLOCK
    s, qn = qproj.shape
    kn = kv.shape[1]
    length = s // dil
    gw = DIL_HEADS_PER_GROUP * HEAD_DIM
    nb = length // DIL_BLOCK
    qb = min(qb, nb)
    rows = qb * DIL_BLOCK
    heads = range(group * DIL_HEADS_PER_GROUP, (group + 1) * DIL_HEADS_PER_GROUP)
    slopes = tuple(2.0 ** (-8.0 * (h + 1) / DIL_HEADS) for h in heads)
    qv = qproj.reshape(length, dil * qn)
    kvv = kv.reshape(length, dil * kn)
    qcb, kcb, vcb = qn // gw, kn // gw, DIL_WIDTH // gw

    def cur(per_stream, off):
        return pl.BlockSpec((rows, gw), lambda c, n: (n, c * per_stream + off))

    def prev(per_stream, off):
        return pl.BlockSpec((DIL_BLOCK, gw), lambda c, n: (jnp.maximum(n * qb - 1, 0), c * per_stream + off))

    out_spec = pl.BlockSpec((rows, gw), lambda c, n: (n, c))
    o, lse = pl.pallas_call(
        functools.partial(_dil_attn_body, dil=dil, slopes=slopes, qb=qb),
        grid=(dil, nb // qb),
        in_specs=[cur(qcb, group), cur(kcb, group), prev(kcb, group),
                  cur(kcb, vcb + group), prev(kcb, vcb + group)],
        out_specs=[out_spec, out_spec],
        out_shape=[jax.ShapeDtypeStruct((length, dil * gw), BF16),
                   jax.ShapeDtypeStruct((length, dil * gw), F32)],
        compiler_params=_cparams(("parallel", "arbitrary")),
        name=f"dilated_attention_g{group}",
    )(qv, kvv, kvv, kvv, kvv)
    return o.reshape(s, gw), lse.reshape(s, gw)


def _ffn(x, g_pre, g_post, w_in, w_out):
    d = w_in.shape[0]
    zc = jnp.zeros((d, D_FF_PAD - D_FF), BF16)
    w_gu = jnp.concatenate([w_in[:, :D_FF].astype(BF16), zc, w_in[:, D_FF:].astype(BF16), zc], axis=1)
    w_o = jnp.concatenate([w_out.astype(BF16), jnp.zeros((D_FF_PAD - D_FF, d), BF16)], axis=0)
    return ffn(x, g_pre, g_post, w_gu, w_o, tm=1024, tf=256)


def _memory_kv(mem, gain, w):
    return prenorm_matmul(mem, gain, w.astype(BF16), tm=mem.shape[0], tn=512)


def _deltanet_layer(x, g_pre, g_post, mem_kv, w_in, conv_w, a_log, dt_bias, o_norm, w_out):
    qkvz = 4 * DN_WIDTH
    w_main = jnp.concatenate([w_in[:, :qkvz], w_in[:, qkvz + 2 * DN_HEADS:]], axis=1).astype(BF16)
    gate_w = lambda lo: jnp.pad(w_in[:, lo:lo + DN_HEADS].T, ((0, GATE_ROWS - DN_HEADS), (0, 0)))
    w_ab_t = jnp.concatenate([gate_w(qkvz), gate_w(qkvz + DN_HEADS)], axis=0).astype(BF16)
    proj, ab_t = dn_in_proj(x, g_pre, w_main, w_ab_t, tm=1024, tn=512)
    gc, beta, gdk = dn_gates(ab_t, a_log, dt_bias, tl=1024)
    o = delta_core(proj, conv_w.astype(F32), gc, beta, gdk, o_norm, tc=256, hb=4)
    o_mem = mem_attention(proj, qkvz // MEM_WIDTH, mem_kv, tm=512)
    w_out = w_out.astype(BF16)
    return dn_out_proj(o, o_mem, w_out[:DN_WIDTH], w_out[DN_WIDTH:], x, g_post, tm=512)


def _dilated_layer(x, g_pre, g_post, kv, mem_kv, w_in, w_out):
    proj = prenorm_matmul(x, g_pre, w_in.astype(BF16), tm=1024, tn=512)
    os, lses = [], []
    for gi in range(len(DIL_GROUPS)):
        o, lse = dilated_group_attention(proj, kv, gi, qb=4)
        os.append(o)
        lses.append(lse)
    o_mem = mem_attention(proj, DIL_WIDTH // MEM_WIDTH, mem_kv, tm=512)
    w_out = w_out.astype(BF16)
    return dil_out_proj(os, lses, o_mem, w_out[:DIL_OUT], w_out[DIL_OUT:], x, g_post, tm=512)


def kernel(x, mem, norm_gains, ffn_w_in, ffn_w_out, mem_norm_gain, w_mem_kv, dn_w_in, dn_conv, dn_a_log,
           dn_dt_bias, dn_o_norm, dn_w_out, kv_norm_gain, w_kv, dil_w_in, dil_w_out):
    batch, seq, d = x.shape
    assert batch == 1
    depth = norm_gains.shape[0]
    n_a = dn_w_in.shape[0]
    xs = x.reshape(seq, d)
    mem2 = mem.reshape(mem.shape[1], d)
    kv = None
    for l in range(depth):
        gains = norm_gains[l]
        if l == n_a:
            kv = prenorm_matmul(xs, kv_norm_gain, w_kv.astype(BF16), tm=1024, tn=512)
        xs = _ffn(xs, gains[0], gains[1], ffn_w_in[l, 0], ffn_w_out[l, 0])
        mem_kv = _memory_kv(mem2, mem_norm_gain[l], w_mem_kv[l])
        if l < n_a:
            xs = _deltanet_layer(xs, gains[2], gains[3], mem_kv, dn_w_in[l], dn_conv[l], dn_a_log[l],
                                 dn_dt_bias[l], dn_o_norm[l], dn_w_out[l])
        else:
            i = l - n_a
            xs = _dilated_layer(xs, gains[2], gains[3], kv, mem_kv, dil_w_in[i], dil_w_out[i])
        xs = _ffn(xs, gains[4], gains[5], ffn_w_in[l, 1], ffn_w_out[l, 1])
    return xs.reshape(batch, seq, d)
```

```python
import functools
import math

import jax
import jax.numpy as jnp
from jax import lax
from jax.experimental import pallas as pl
from jax.experimental.pallas import tpu as pltpu

F32 = jnp.float32
BF16 = jnp.bfloat16

D_MODEL = 2048
HEAD_DIM = 128
DN_HEADS = 12
DN_WIDTH = DN_HEADS * HEAD_DIM
CONV_K = 4
CHUNK = 64
MEM_HEADS = 4
MEM_WIDTH = MEM_HEADS * HEAD_DIM
DIL_GROUPS = ((128, 1), (512, 4), (2048, 16))
DIL_HEADS_PER_GROUP = 4
DIL_HEADS = DIL_HEADS_PER_GROUP * len(DIL_GROUPS)
DIL_WIDTH = DIL_HEADS * HEAD_DIM
DIL_OUT = DIL_HEADS_PER_GROUP * HEAD_DIM
DIL_BLOCK = 128
D_FF = ((8 * D_MODEL // 3 + 127) // 128) * 128
NORM_EPS = 1e-6

V7X_LANES = 128
V7X_BF16_SUBLANES = 16
V7X_VMEM_LIMIT_BYTES = 56 * 1024 * 1024

FF_TILE = 512
D_FF_PAD = -(-D_FF // FF_TILE) * FF_TILE
GATE_ROWS = 16
SUPER = 2 * CHUNK


def _cparams(sem):
    return pltpu.CompilerParams(dimension_semantics=sem, vmem_limit_bytes=V7X_VMEM_LIMIT_BYTES)


def _rms_scale(x):
    return lax.rsqrt(jnp.mean(x * x, axis=-1, keepdims=True) + NORM_EPS)


def _silu(x):
    return x * jax.nn.sigmoid(x)


def _block_id(idx, size):
    assert size & (size - 1) == 0
    return lax.shift_right_logical(idx, size.bit_length() - 1)


def _dot(a, b):
    return jnp.dot(a, b, preferred_element_type=F32)


def _dot_nt(a, b):
    return lax.dot_general(a, b, (((1,), (1,)), ((), ())), preferred_element_type=F32)


def _prenorm_matmul_body(x_ref, g_ref, w_ref, o_ref, hn_ref):
    @pl.when(pl.program_id(1) == 0)
    def _():
        x = x_ref[...]
        hn_ref[...] = (x * _rms_scale(x) * g_ref[...]).astype(BF16)

    o_ref[...] = _dot(hn_ref[...], w_ref[...].astype(BF16)).astype(o_ref.dtype)


def _wspec(lead, block, index_fn):
    return pl.BlockSpec((None,) * len(lead) + block, lambda *g: lead + index_fn(*g))


def prenorm_matmul(x, gain, w, lead=(), *, tm, tn, out_dtype=BF16):
    m, d = x.shape
    n = w.shape[-1]
    return pl.pallas_call(
        _prenorm_matmul_body,
        grid=(m // tm, n // tn),
        in_specs=[
            pl.BlockSpec((tm, d), lambda i, j: (i, 0)),
            pl.BlockSpec((1, d), lambda i, j: (0, 0)),
            _wspec(lead, (d, tn), lambda i, j: (0, j)),
        ],
        out_specs=pl.BlockSpec((tm, tn), lambda i, j: (i, j)),
        out_shape=jax.ShapeDtypeStruct((m, n), out_dtype),
        scratch_shapes=[pltpu.VMEM((tm, d), BF16)],
        compiler_params=_cparams(("parallel", "arbitrary")),
        name="prenorm_matmul",
    )(x, gain.reshape(1, d), w)


def _dn_in_proj_body(x_ref, g_ref, w_ref, wmq_ref, wab_ref, o_ref, ab_ref, hn_ref, *, n_main):
    j = pl.program_id(1)

    @pl.when(j == 0)
    def _():
        x = x_ref[...]
        hn = (x * _rms_scale(x) * g_ref[...]).astype(BF16)
        hn_ref[...] = hn
        ab_ref[...] = _dot_nt(wab_ref[...], hn)

    @pl.when(j < n_main)
    def _():
        o_ref[...] = _dot(hn_ref[...], w_ref[...].astype(BF16)).astype(o_ref.dtype)

    @pl.when(j == n_main)
    def _():
        o_ref[...] = _dot(hn_ref[...], wmq_ref[...].astype(BF16)).astype(o_ref.dtype)


def dn_in_proj(x, gain, w_in, lead, w_mq, w_ab_t, *, tm, tn):
    m, d = x.shape
    n_main = (4 * DN_WIDTH) // tn
    assert n_main * tn == 4 * DN_WIDTH and w_mq.shape == (d, tn)
    r = w_ab_t.shape[0]
    return pl.pallas_call(
        functools.partial(_dn_in_proj_body, n_main=n_main),
        grid=(m // tm, n_main + 1),
        in_specs=[
            pl.BlockSpec((tm, d), lambda i, j: (i, 0)),
            pl.BlockSpec((1, d), lambda i, j: (0, 0)),
            _wspec(lead, (d, tn), lambda i, j: (0, jnp.minimum(j, n_main - 1))),
            pl.BlockSpec((d, tn), lambda i, j: (0, 0)),
            pl.BlockSpec((r, d), lambda i, j: (0, 0)),
        ],
        out_specs=[
            pl.BlockSpec((tm, tn), lambda i, j: (i, j)),
            pl.BlockSpec((r, tm), lambda i, j: (0, i)),
        ],
        out_shape=[
            jax.ShapeDtypeStruct((m, (n_main + 1) * tn), BF16),
            jax.ShapeDtypeStruct((r, m), F32),
        ],
        scratch_shapes=[pltpu.VMEM((tm, d), BF16)],
        compiler_params=_cparams(("parallel", "arbitrary")),
        name="dn_in_proj",
    )(x, gain.reshape(1, d), w_in, w_mq, w_ab_t)


def _postnorm_residual(x, y, gain, scale):
    return x + scale * (y * _rms_scale(y) * gain)


FFN_ROW_CHUNK = 128


def _ffn_body(x_ref, gpre_ref, ga_ref, gb_ref, ua_ref, ub_ref, wa_ref, wb_ref, gpost_ref, o_ref, hn_ref,
              *, odd_tail):
    j = pl.program_id(1)
    last = pl.num_programs(1) - 1
    tm = x_ref.shape[0]

    def row_chunks(fn):
        def body(r, carry):
            fn(pl.ds(pl.multiple_of(r * FFN_ROW_CHUNK, FFN_ROW_CHUNK), FFN_ROW_CHUNK))
            return carry
        lax.fori_loop(0, tm // FFN_ROW_CHUNK, body, 0)

    @pl.when(j == 0)
    def _():
        def prologue(rows):
            x = x_ref[rows, :]
            hn_ref[rows, :] = (x * _rms_scale(x) * gpre_ref[...]).astype(BF16)
            o_ref[rows, :] = jnp.zeros((FFN_ROW_CHUNK, o_ref.shape[1]), F32)
        row_chunks(prologue)

    w_gu = jnp.concatenate([ga_ref[...], gb_ref[...], ua_ref[...], ub_ref[...]], axis=1).astype(BF16)
    wb = wb_ref[...]
    if odd_tail:
        wb = jnp.where(j == last, 0.0, wb)
    w_o = jnp.concatenate([wa_ref[...], wb], axis=0).astype(BF16)
    gu = _dot(hn_ref[...], w_gu)
    half = gu.shape[1] // 2
    act = (_silu(gu[:, :half]) * gu[:, half:]).astype(BF16)
    o_ref[...] += _dot(act, w_o)

    @pl.when(j == last)
    def _():
        def epilogue(rows):
            o_ref[rows, :] = _postnorm_residual(x_ref[rows, :], o_ref[rows, :], gpost_ref[...], 0.5)
        row_chunks(epilogue)


def ffn(x, g_pre, g_post, w_in, w_out, lead, *, tm):
    m, d = x.shape
    ff = w_out.shape[-2]
    blk = V7X_LANES
    nblk = ff // blk
    assert nblk * blk == ff and w_in.shape[-2:] == (d, 2 * ff)
    nf = -(-nblk // 2)
    second = lambda j: jnp.minimum(2 * j + 1, nblk - 1)
    col = lambda f: _wspec(lead, (d, blk), lambda i, j: (0, f(j)))
    row = lambda f: _wspec(lead, (blk, d), lambda i, j: (f(j), 0))
    vec = pl.BlockSpec((1, d), lambda i, j: (0, 0))
    return pl.pallas_call(
        functools.partial(_ffn_body, odd_tail=bool(nblk % 2)),
        grid=(m // tm, nf),
        in_specs=[
            pl.BlockSpec((tm, d), lambda i, j: (i, 0)),
            vec,
            col(lambda j: 2 * j), col(second),
            col(lambda j: nblk + 2 * j), col(lambda j: nblk + second(j)),
            row(lambda j: 2 * j), row(second),
            vec,
        ],
        out_specs=pl.BlockSpec((tm, d), lambda i, j: (i, 0)),
        out_shape=jax.ShapeDtypeStruct((m, d), F32),
        scratch_shapes=[pltpu.VMEM((tm, d), BF16)],
        compiler_params=_cparams(("parallel", "arbitrary")),
        name="ffn",
    )(x, g_pre.reshape(1, d), w_in, w_in, w_in, w_in, w_out, w_out, g_post.reshape(1, d))


def _dn_out_body(o_ref, om_ref, w_ref, x_ref, g_ref, out_ref):
    k = pl.program_id(1)
    last = pl.num_programs(1) - 1
    w = w_ref[...].astype(BF16)

    @pl.when(k == 0)
    def _():
        out_ref[...] = _dot(o_ref[...], w)

    @pl.when((k > 0) & (k < last))
    def _():
        out_ref[...] += _dot(o_ref[...], w)

    @pl.when(k == last)
    def _():
        y = out_ref[...] + _dot(om_ref[...], w)
        out_ref[...] = _postnorm_residual(x_ref[...], y, g_ref[...], 1.0)


def dn_out_proj(o, o_mem, w_out, lead, x, gain, *, tm):
    m, d = x.shape
    k1, k2 = o.shape[1], o_mem.shape[1]
    n1 = k1 // k2
    assert n1 * k2 == k1 and w_out.shape[-2:] == (k1 + k2, d)
    return pl.pallas_call(
        _dn_out_body,
        grid=(m // tm, n1 + 1),
        in_specs=[
            pl.BlockSpec((tm, k2), lambda i, k: (i, jnp.minimum(k, n1 - 1))),
            pl.BlockSpec((tm, k2), lambda i, k: (i, 0)),
            _wspec(lead, (k2, d), lambda i, k: (k, 0)),
            pl.BlockSpec((tm, d), lambda i, k: (i, 0)),
            pl.BlockSpec((1, d), lambda i, k: (0, 0)),
        ],
        out_specs=pl.BlockSpec((tm, d), lambda i, k: (i, 0)),
        out_shape=jax.ShapeDtypeStruct((m, d), F32),
        compiler_params=_cparams(("parallel", "arbitrary")),
        name="dn_out_proj",
    )(o, o_mem, w_out, x, gain.reshape(1, d))


def _dil_out_body(o0_ref, o1_ref, o2_ref, l0_ref, l1_ref, l2_ref, om_ref, w1_ref, w2_ref, x_ref, g_ref,
                  out_ref):
    l0, l1, l2 = l0_ref[...], l1_ref[...], l2_ref[...]
    mx = jnp.maximum(jnp.maximum(l0, l1), l2)
    e0, e1, e2 = jnp.exp(l0 - mx), jnp.exp(l1 - mx), jnp.exp(l2 - mx)
    o = (e0 * o0_ref[...].astype(F32) + e1 * o1_ref[...].astype(F32) + e2 * o2_ref[...].astype(F32))
    o = o / (e0 + e1 + e2)
    y = _dot(o.astype(BF16), w1_ref[...].astype(BF16)) + _dot(om_ref[...], w2_ref[...].astype(BF16))
    out_ref[...] = _postnorm_residual(x_ref[...], y, g_ref[...], 1.0)


def dil_out_proj(os, lses, o_mem, w_out, lead, x, gain, *, tm):
    m, d = x.shape
    k1, k2 = os[0].shape[1], o_mem.shape[1]
    assert k1 == k2 and w_out.shape[-2:] == (k1 + k2, d)
    row = lambda kk: pl.BlockSpec((tm, kk), lambda i: (i, 0))
    return pl.pallas_call(
        _dil_out_body,
        grid=(m // tm,),
        in_specs=[row(k1)] * 3 + [row(k1)] * 3 + [
            row(k2),
            _wspec(lead, (k1, d), lambda i: (0, 0)),
            _wspec(lead, (k2, d), lambda i: (1, 0)),
            row(d),
            pl.BlockSpec((1, d), lambda i: (0, 0)),
        ],
        out_specs=row(d),
        out_shape=jax.ShapeDtypeStruct((m, d), F32),
        compiler_params=_cparams(("parallel",)),
        name="dil_out_proj",
    )(*os, *lses, o_mem, w_out, w_out, x, gain.reshape(1, d))


def _mem_attn_body(q_ref, k_ref, v_ref, o_ref):
    for h in range(MEM_HEADS):
        cs = slice(h * HEAD_DIM, (h + 1) * HEAD_DIM)
        s = _dot_nt(q_ref[:, cs], k_ref[:, cs]) * (HEAD_DIM ** -0.5)
        e = jnp.exp(s - jnp.max(s, axis=-1, keepdims=True))
        den = jnp.sum(e, axis=-1, keepdims=True)
        o_ref[:, cs] = (_dot(e.astype(BF16), v_ref[:, cs]) / den).astype(o_ref.dtype)


def mem_attention(proj, q_col_block, mem_kv, *, tm):
    m = proj.shape[0]
    mlen = mem_kv.shape[0]
    return pl.pallas_call(
        _mem_attn_body,
        grid=(m // tm,),
        in_specs=[
            pl.BlockSpec((tm, MEM_WIDTH), lambda i: (i, q_col_block)),
            pl.BlockSpec((mlen, MEM_WIDTH), lambda i: (0, 0)),
            pl.BlockSpec((mlen, MEM_WIDTH), lambda i: (0, 1)),
        ],
        out_specs=pl.BlockSpec((tm, MEM_WIDTH), lambda i: (i, 0)),
        out_shape=jax.ShapeDtypeStruct((m, MEM_WIDTH), BF16),
        compiler_params=_cparams(("parallel",)),
        name="mem_attention",
    )(proj, mem_kv, mem_kv)


def _gate_body(ab_ref, alog_ref, dtb_ref, gc_ref, beta_ref, gdk_ref):
    tl = ab_ref.shape[1]
    a = ab_ref[0:GATE_ROWS, :]
    b = ab_ref[GATE_ROWS:2 * GATE_ROWS, :]
    z = a + dtb_ref[...]
    softplus = jnp.maximum(z, 0.0) + jnp.log1p(jnp.exp(-jnp.abs(z)))
    g = -jnp.exp(alog_ref[...]) * softplus
    beta_ref[...] = jax.nn.sigmoid(b)
    r = lax.broadcasted_iota(jnp.int32, (V7X_LANES, V7X_LANES), 0)
    c = lax.broadcasted_iota(jnp.int32, (V7X_LANES, V7X_LANES), 1)
    same = _block_id(r, CHUNK) == _block_id(c, CHUNK)
    upper = jnp.where(same & (r <= c), 1.0, 0.0).astype(F32)
    ones = jnp.where(same, 1.0, 0.0).astype(F32)
    for s in range(tl // V7X_LANES):
        ls = slice(s * V7X_LANES, (s + 1) * V7X_LANES)
        gs = g[:, ls]
        gc = jnp.dot(gs, upper, preferred_element_type=F32, precision=lax.Precision.HIGHEST)
        gend = jnp.dot(gs, ones, preferred_element_type=F32, precision=lax.Precision.HIGHEST)
        gc_ref[:, ls] = gc
        gdk_ref[:, ls] = gend - gc


def dn_gates(ab_t, a_log, dt_bias, *, tl):
    r, s = ab_t.shape
    pad = lambda v: jnp.pad(v.astype(F32), (0, GATE_ROWS - v.shape[0])).reshape(GATE_ROWS, 1)
    out = jax.ShapeDtypeStruct((GATE_ROWS, s), F32)
    return pl.pallas_call(
        _gate_body,
        grid=(s // tl,),
        in_specs=[
            pl.BlockSpec((r, tl), lambda i: (0, i)),
            pl.BlockSpec((GATE_ROWS, 1), lambda i: (0, 0)),
            pl.BlockSpec((GATE_ROWS, 1), lambda i: (0, 0)),
        ],
        out_specs=[pl.BlockSpec((GATE_ROWS, tl), lambda i: (0, i))] * 3,
        out_shape=[out, out, out],
        compiler_params=_cparams(("parallel",)),
        name="dn_gates",
    )(ab_t, pad(a_log), pad(dt_bias))


def _col_bcast(row):
    n = row.shape[1]
    return jnp.broadcast_to(row, (n, n)).T


def _unit_lower_inverses(nmats, ri, ci):
    def quarter(b):
        same_block = _block_id(ri, 2 * b) == _block_id(ci, 2 * b)
        return same_block & ((ri & (2 * b - 1)) >= b) & ((ci & (2 * b - 1)) < b)

    eye = jnp.where(ri == ci, 1.0, 0.0).astype(F32)
    first = quarter(1)
    xs = [eye - jnp.where(first, n, 0.0) for n in nmats]
    b = 2
    while b < CHUNK:
        mask = quarter(b)
        ys = [_dot(jnp.where(mask, n, 0.0).astype(BF16), x.astype(BF16)) for n, x in zip(nmats, xs)]
        xs = [x - _dot(x.astype(BF16), y.astype(BF16)) for x, y in zip(xs, ys)]
        b *= 2
    return xs


def _delta_body(q_ref, k_ref, v_ref, z_ref, qh_ref, kh_ref, vh_ref, wq_ref, wk_ref, wv_ref,
                gc_ref, beta_ref, gdk_ref, on_ref, o_ref, state_ref, buf_ref, *, tc, hb):
    t = pl.program_id(1)
    halo = V7X_BF16_SUBLANES
    n_super = tc // SUPER
    n_chunk = SUPER // CHUNK

    @pl.when(t == 0)
    def _():
        state_ref[...] = jnp.zeros_like(state_ref)

    states = [state_ref[h] for h in range(hb)]

    ri = lax.broadcasted_iota(jnp.int32, (SUPER, SUPER), 0)
    ci = lax.broadcasted_iota(jnp.int32, (SUPER, SUPER), 1)
    same = _block_id(ri, CHUNK) == _block_id(ci, CHUNK)
    causal = same & (ri >= ci)
    strict = same & (ri > ci)

    def conv_silu(slot, cur_ref, halo_ref, w_ref, cs):
        buf_ref[slot, 0:halo, :] = jnp.where(t == 0, 0.0, halo_ref[:, cs].astype(F32))
        buf_ref[slot, halo:halo + tc, :] = cur_ref[:, cs].astype(F32)
        y = jnp.zeros((tc, HEAD_DIM), F32)
        for j in range(CONV_K):
            off = halo - (CONV_K - 1) + j
            y = y + w_ref[j:j + 1, cs] * buf_ref[slot, off:off + tc, :]
        return _silu(y)

    def l2n(x):
        return x * lax.rsqrt(jnp.sum(x * x, axis=-1, keepdims=True) + NORM_EPS)

    heads = range(hb)
    lanes = [slice(h * HEAD_DIM, (h + 1) * HEAD_DIM) for h in heads]
    q_all = [l2n(conv_silu(3 * h, q_ref, qh_ref, wq_ref, lanes[h])) * (HEAD_DIM ** -0.5) for h in heads]
    k_all = [l2n(conv_silu(3 * h + 1, k_ref, kh_ref, wk_ref, lanes[h])) for h in heads]
    v_all = [conv_silu(3 * h + 2, v_ref, vh_ref, wv_ref, lanes[h]) for h in heads]

    units = [(h, s) for s in range(n_super) for h in heads]
    rows = lambda s: slice(s * SUPER, (s + 1) * SUPER)
    q = [q_all[h][rows(s)] for h, s in units]
    k = [k_all[h][rows(s)] for h, s in units]
    v = [v_all[h][rows(s)] for h, s in units]
    g_row = [gc_ref[h, :, rows(s)] for h, s in units]
    gdk_row = [gdk_ref[h, :, rows(s)] for h, s in units]
    g_col = [_col_bcast(g) for g in g_row]
    gdk_col = [_col_bcast(g) for g in gdk_row]
    beta_col = [_col_bcast(beta_ref[h, :, rows(s)]) for h, s in units]
    decay = [jnp.exp(jnp.where(causal, gc - jnp.broadcast_to(gr, (SUPER, SUPER)), -jnp.inf))
             for gc, gr in zip(g_col, g_row)]
    exp_g = [jnp.exp(g) for g in g_col]
    g_last = [jnp.broadcast_to(jnp.exp(gr + gd), (SUPER, SUPER)) for gr, gd in zip(g_row, gdk_row)]

    kb = [x.astype(BF16) for x in k]
    kk = [_dot_nt(x, x) for x in kb]
    qk = [_dot_nt(a.astype(BF16), b) for a, b in zip(q, kb)]
    nmat = [jnp.where(strict, b * x * d, 0.0) for b, x, d in zip(beta_col, kk, decay)]
    tinv = _unit_lower_inverses(nmat, ri, ci)
    rhs = [jnp.concatenate([kx * b * e, vx * b], axis=1).astype(BF16)
           for kx, vx, b, e in zip(k, v, beta_col, exp_g)]
    wu = [_dot(x.astype(BF16), r).astype(BF16) for x, r in zip(tinv, rhs)]
    attn = [(a * d).astype(BF16) for a, d in zip(qk, decay)]
    a_wu = [_dot(a, x) for a, x in zip(attn, wu)]
    q_eff = [(qx * e - aw[:, :HEAD_DIM]).astype(BF16) for qx, e, aw in zip(q, exp_g, a_wu)]
    o_intra = [aw[:, HEAD_DIM:] for aw in a_wu]
    kd_t = [(kx * jnp.exp(g)).T for kx, g in zip(k, gdk_col)]
    in_chunk = [_block_id(ci, CHUNK) == c for c in range(n_chunk)]
    pr = [[_dot(jnp.where(in_chunk[c], kt, 0.0).astype(BF16), x) for c in range(n_chunk)]
          for kt, x in zip(kd_t, wu)]

    outs = [[] for _ in heads]
    for s in range(n_super):
        for c in range(n_chunk):
            cr = slice(c * CHUNK, (c + 1) * CHUNK)
            for h in heads:
                u = s * hb + h
                lhs = jnp.concatenate([q_eff[u][cr], pr[u][c][:, :HEAD_DIM].astype(BF16)], axis=0)
                r = _dot(lhs, states[h].astype(BF16))
                outs[h].append(r[:CHUNK] + o_intra[u][cr])
                gl = jnp.broadcast_to(g_last[u][:, c * CHUNK:c * CHUNK + 1], (HEAD_DIM, HEAD_DIM))
                states[h] = states[h] * gl + (pr[u][c][:, HEAD_DIM:] - r[CHUNK:])

    o_norm = on_ref[...]
    for h in heads:
        o = jnp.concatenate(outs[h], axis=0)
        o = o * _rms_scale(o) * o_norm * _silu(z_ref[:, lanes[h]].astype(F32))
        o_ref[:, lanes[h]] = o.astype(o_ref.dtype)
    for h in heads:
        state_ref[h] = states[h]


def delta_core(proj, conv_w, gc, beta, gdk, o_norm, *, tc, hb):
    s = proj.shape[0]
    nh = DN_HEADS
    assert nh % hb == 0
    ng = nh // hb
    halo = V7X_BF16_SUBLANES
    nhalo = tc // halo
    width = hb * HEAD_DIM

    def cur(sec):
        return pl.BlockSpec((tc, width), lambda g, t: (t, sec * ng + g))

    def prev(sec):
        return pl.BlockSpec((halo, width), lambda g, t: (jnp.maximum(t * nhalo - 1, 0), sec * ng + g))

    def cw(sec):
        return pl.BlockSpec((CONV_K, width), lambda g, t: (0, sec * ng + g))

    gate = pl.BlockSpec((hb, 1, tc), lambda g, t: (g, 0, t))
    g3 = lambda a: a.reshape(GATE_ROWS, 1, s)
    return pl.pallas_call(
        functools.partial(_delta_body, tc=tc, hb=hb),
        grid=(ng, s // tc),
        in_specs=[cur(0), cur(1), cur(2), cur(3), prev(0), prev(1), prev(2),
                  cw(0), cw(1), cw(2), gate, gate, gate,
                  pl.BlockSpec((1, HEAD_DIM), lambda g, t: (0, 0))],
        out_specs=pl.BlockSpec((tc, width), lambda g, t: (t, g)),
        out_shape=jax.ShapeDtypeStruct((s, nh * HEAD_DIM), BF16),
        scratch_shapes=[pltpu.VMEM((hb, HEAD_DIM, HEAD_DIM), F32),
                        pltpu.VMEM((3 * hb, tc + halo, HEAD_DIM), F32)],
        compiler_params=_cparams(("parallel", "arbitrary")),
        name="delta_core",
    )(proj, proj, proj, proj, proj, proj, proj, conv_w, conv_w, conv_w,
      g3(gc), g3(beta), g3(gdk), o_norm.reshape(1, HEAD_DIM))


def _dil_attn_body(q_ref, kc_ref, kp_ref, vc_ref, vp_ref, o_ref, l_ref, *, dil, slopes, qb):
    n = pl.program_id(1)
    blk = DIL_BLOCK
    ri = lax.broadcasted_iota(jnp.int32, (blk, 2 * blk), 0)
    ci = lax.broadcasted_iota(jnp.int32, (blk, 2 * blk), 1)
    rel = ri + blk - ci
    window = (rel >= 0) & (rel <= blk)
    dist = (rel * dil).astype(F32)
    for b in range(qb):
        rows = slice(b * blk, (b + 1) * blk)
        if b == 0:
            valid = window & ((ci >= blk) | (n > 0))
        else:
            valid = window
        for h in range(DIL_HEADS_PER_GROUP):
            cs = slice(h * HEAD_DIM, (h + 1) * HEAD_DIM)
            if b == 0:
                kp, vp = kp_ref[:, cs], vp_ref[:, cs]
            else:
                prows = slice((b - 1) * blk, b * blk)
                kp, vp = kc_ref[prows, cs], vc_ref[prows, cs]
            keys = jnp.concatenate([kp, kc_ref[rows, cs]], axis=0)
            vals = jnp.concatenate([vp, vc_ref[rows, cs]], axis=0)
            s = _dot_nt(q_ref[rows, cs], keys) * (HEAD_DIM ** -0.5)
            s = jnp.where(valid, s - slopes[h] * dist, -jnp.inf)
            mx = jnp.max(s, axis=-1, keepdims=True)
            e = jnp.exp(s - mx)
            den = jnp.sum(e, axis=-1, keepdims=True)
            o = _dot(e.astype(BF16), vals) / den
            o_ref[rows, cs] = o.astype(o_ref.dtype)
            l_ref[rows, cs] = jnp.broadcast_to(mx + jnp.log(den), (blk, HEAD_DIM))


def dilated_group_attention(qproj, kv, group, *, qb):
    win, dil = DIL_GROUPS[group]
    assert win // dil == DIL_BLOCK
    s, qn = qproj.shape
    kn = kv.shape[1]
    length = s // dil
    gw = DIL_HEADS_PER_GROUP * HEAD_DIM
    nb = length // DIL_BLOCK
    qb = min(qb, nb)
    rows = qb * DIL_BLOCK
    heads = range(group * DIL_HEADS_PER_GROUP, (group + 1) * DIL_HEADS_PER_GROUP)
    slopes = tuple(2.0 ** (-8.0 * (h + 1) / DIL_HEADS) for h in heads)
    qv = qproj.reshape(length, dil * qn)
    kvv = kv.reshape(length, dil * kn)
    qcb, kcb, vcb = qn // gw, kn // gw, DIL_WIDTH // gw

    def cur(per_stream, off):
        return pl.BlockSpec((rows, gw), lambda c, n: (n, c * per_stream + off))

    def prev(per_stream, off):
        return pl.BlockSpec((DIL_BLOCK, gw), lambda c, n: (jnp.maximum(n * qb - 1, 0), c * per_stream + off))

    out_spec = pl.BlockSpec((rows, gw), lambda c, n: (n, c))
    o, lse = pl.pallas_call(
        functools.partial(_dil_attn_body, dil=dil, slopes=slopes, qb=qb),
        grid=(dil, nb // qb),
        in_specs=[cur(qcb, group), cur(kcb, group), prev(kcb, group),
                  cur(kcb, vcb + group), prev(kcb, vcb + group)],
        out_specs=[out_spec, out_spec],
        out_shape=[jax.ShapeDtypeStruct((length, dil * gw), BF16),
                   jax.ShapeDtypeStruct((length, dil * gw), F32)],
        compiler_params=_cparams(("parallel", "arbitrary")),
        name=f"dilated_attention_g{group}",
    )(qv, kvv, kvv, kvv, kvv)
    return o.reshape(s, gw), lse.reshape(s, gw)


def _deltanet_layer(x, g_pre, g_post, mem_kv, dn_w_in, l, conv_w, a_log, dt_bias, o_norm, dn_w_out):
    qkvz = 4 * DN_WIDTH
    w_in = dn_w_in[l]
    w_mq = w_in[:, qkvz + 2 * DN_HEADS:]
    gate_w = lambda lo: jnp.pad(w_in[:, lo:lo + DN_HEADS].T, ((0, GATE_ROWS - DN_HEADS), (0, 0)))
    w_ab_t = jnp.concatenate([gate_w(qkvz), gate_w(qkvz + DN_HEADS)], axis=0).astype(BF16)
    proj, ab_t = dn_in_proj(x, g_pre, dn_w_in, (l,), w_mq, w_ab_t, tm=1024, tn=MEM_WIDTH)
    gc, beta, gdk = dn_gates(ab_t, a_log, dt_bias, tl=1024)
    o = delta_core(proj, conv_w.astype(F32), gc, beta, gdk, o_norm, tc=256, hb=4)
    o_mem = mem_attention(proj, qkvz // MEM_WIDTH, mem_kv, tm=512)
    return dn_out_proj(o, o_mem, dn_w_out, (l,), x, g_post, tm=512)


def _dilated_layer(x, g_pre, g_post, kv, mem_kv, dil_w_in, i, dil_w_out):
    proj = prenorm_matmul(x, g_pre, dil_w_in, (i,), tm=1024, tn=512)
    os, lses = [], []
    for gi in range(len(DIL_GROUPS)):
        o, lse = dilated_group_attention(proj, kv, gi, qb=4)
        os.append(o)
        lses.append(lse)
    o_mem = mem_attention(proj, DIL_WIDTH // MEM_WIDTH, mem_kv, tm=512)
    return dil_out_proj(os, lses, o_mem, dil_w_out, (i,), x, g_post, tm=512)


def kernel(x, mem, norm_gains, ffn_w_in, ffn_w_out, mem_norm_gain, w_mem_kv, dn_w_in, dn_conv, dn_a_log,
           dn_dt_bias, dn_o_norm, dn_w_out, kv_norm_gain, w_kv, dil_w_in, dil_w_out):
    batch, seq, d = x.shape
    assert batch == 1
    depth = norm_gains.shape[0]
    n_a = dn_w_in.shape[0]
    xs = x.reshape(seq, d)
    mem2 = mem.reshape(mem.shape[1], d)
    kv = None
    for l in range(depth):
        gains = norm_gains[l]
        if l == n_a:
            kv = prenorm_matmul(xs, kv_norm_gain, w_kv, tm=1024, tn=512)
        xs = ffn(xs, gains[0], gains[1], ffn_w_in, ffn_w_out, (l, 0), tm=1024)
        mem_kv = prenorm_matmul(mem2, mem_norm_gain[l], w_mem_kv, (l,), tm=mem2.shape[0], tn=512)
        if l < n_a:
            xs = _deltanet_layer(xs, gains[2], gains[3], mem_kv, dn_w_in, l, dn_conv[l], dn_a_log[l],
                                 dn_dt_bias[l], dn_o_norm[l], dn_w_out)
        else:
            i = l - n_a
            xs = _dilated_layer(xs, gains[2], gains[3], kv, mem_kv, dil_w_in, i, dil_w_out)
        xs = ffn(xs, gains[4], gains[5], ffn_w_in, ffn_w_out, (l, 1), tm=1024)
    return xs.reshape(batch, seq, d)
```

```python
import functools
import math

import jax
import jax.numpy as jnp
from jax import lax
from jax.experimental import pallas as pl
from jax.experimental.pallas import tpu as pltpu

F32 = jnp.float32
BF16 = jnp.bfloat16

D_MODEL = 2048
HEAD_DIM = 128
DN_HEADS = 12
DN_WIDTH = DN_HEADS * HEAD_DIM
CONV_K = 4
CHUNK = 64
MEM_HEADS = 4
MEM_WIDTH = MEM_HEADS * HEAD_DIM
DIL_GROUPS = ((128, 1), (512, 4), (2048, 16))
DIL_HEADS_PER_GROUP = 4
DIL_HEADS = DIL_HEADS_PER_GROUP * len(DIL_GROUPS)
DIL_WIDTH = DIL_HEADS * HEAD_DIM
DIL_OUT = DIL_HEADS_PER_GROUP * HEAD_DIM
DIL_BLOCK = 128
D_FF = ((8 * D_MODEL // 3 + 127) // 128) * 128
NORM_EPS = 1e-6

V7X_LANES = 128
V7X_BF16_SUBLANES = 16
V7X_VMEM_LIMIT_BYTES = 56 * 1024 * 1024

FF_TILE = 512
D_FF_PAD = -(-D_FF // FF_TILE) * FF_TILE
GATE_ROWS = 16
SUPER = 2 * CHUNK


def _cparams(sem):
    return pltpu.CompilerParams(dimension_semantics=sem, vmem_limit_bytes=V7X_VMEM_LIMIT_BYTES)


def _rms_scale(x):
    return lax.rsqrt(jnp.mean(x * x, axis=-1, keepdims=True) + NORM_EPS)


def _silu(x):
    return x * jax.nn.sigmoid(x)


def _block_id(idx, size):
    assert size & (size - 1) == 0
    return lax.shift_right_logical(idx, size.bit_length() - 1)


def _dot(a, b):
    return jnp.dot(a, b, preferred_element_type=F32)


def _dot_nt(a, b):
    return lax.dot_general(a, b, (((1,), (1,)), ((), ())), preferred_element_type=F32)


def _prenorm_matmul_body(x_ref, g_ref, w_ref, o_ref, hn_ref):
    @pl.when(pl.program_id(1) == 0)
    def _():
        x = x_ref[...]
        hn_ref[...] = (x * _rms_scale(x) * g_ref[...]).astype(BF16)

    o_ref[...] = _dot(hn_ref[...], w_ref[...].astype(BF16)).astype(o_ref.dtype)


def _wspec(lead, block, index_fn):
    return pl.BlockSpec((None,) * len(lead) + block, lambda *g: lead + index_fn(*g))


def prenorm_matmul(x, gain, w, lead=(), *, tm, tn, out_dtype=BF16):
    m, d = x.shape
    n = w.shape[-1]
    return pl.pallas_call(
        _prenorm_matmul_body,
        grid=(m // tm, n // tn),
        in_specs=[
            pl.BlockSpec((tm, d), lambda i, j: (i, 0)),
            pl.BlockSpec((1, d), lambda i, j: (0, 0)),
            _wspec(lead, (d, tn), lambda i, j: (0, j)),
        ],
        out_specs=pl.BlockSpec((tm, tn), lambda i, j: (i, j)),
        out_shape=jax.ShapeDtypeStruct((m, n), out_dtype),
        scratch_shapes=[pltpu.VMEM((tm, d), BF16)],
        compiler_params=_cparams(("parallel", "arbitrary")),
        name="prenorm_matmul",
    )(x, gain.reshape(1, d), w)


def _dn_in_proj_body(x_ref, g_ref, w_ref, wmq_ref, wab_ref, o_ref, ab_ref, hn_ref, *, n_main):
    j = pl.program_id(1)

    @pl.when(j == 0)
    def _():
        x = x_ref[...]
        hn = (x * _rms_scale(x) * g_ref[...]).astype(BF16)
        hn_ref[...] = hn
        ab_ref[...] = _dot_nt(wab_ref[...], hn)

    @pl.when(j < n_main)
    def _():
        o_ref[...] = _dot(hn_ref[...], w_ref[...].astype(BF16)).astype(o_ref.dtype)

    @pl.when(j == n_main)
    def _():
        o_ref[...] = _dot(hn_ref[...], wmq_ref[...].astype(BF16)).astype(o_ref.dtype)


def dn_in_proj(x, gain, w_in, lead, w_mq, w_ab_t, *, tm, tn):
    m, d = x.shape
    n_main = (4 * DN_WIDTH) // tn
    assert n_main * tn == 4 * DN_WIDTH and w_mq.shape == (d, tn)
    r = w_ab_t.shape[0]
    return pl.pallas_call(
        functools.partial(_dn_in_proj_body, n_main=n_main),
        grid=(m // tm, n_main + 1),
        in_specs=[
            pl.BlockSpec((tm, d), lambda i, j: (i, 0)),
            pl.BlockSpec((1, d), lambda i, j: (0, 0)),
            _wspec(lead, (d, tn), lambda i, j: (0, jnp.minimum(j, n_main - 1))),
            pl.BlockSpec((d, tn), lambda i, j: (0, 0)),
            pl.BlockSpec((r, d), lambda i, j: (0, 0)),
        ],
        out_specs=[
            pl.BlockSpec((tm, tn), lambda i, j: (i, j)),
            pl.BlockSpec((r, tm), lambda i, j: (0, i)),
        ],
        out_shape=[
            jax.ShapeDtypeStruct((m, (n_main + 1) * tn), BF16),
            jax.ShapeDtypeStruct((r, m), F32),
        ],
        scratch_shapes=[pltpu.VMEM((tm, d), BF16)],
        compiler_params=_cparams(("parallel", "arbitrary")),
        name="dn_in_proj",
    )(x, gain.reshape(1, d), w_in, w_mq, w_ab_t)


def _postnorm_residual(x, y, gain, scale):
    return x + scale * (y * _rms_scale(y) * gain)


FFN_ROW_CHUNK = 128


def _ffn_body(x_ref, gpre_ref, ga_ref, gb_ref, ua_ref, ub_ref, wa_ref, wb_ref, gpost_ref, o_ref, hn_ref,
              *, odd_tail):
    j = pl.program_id(1)
    last = pl.num_programs(1) - 1
    tm = x_ref.shape[0]

    def row_chunks(fn):
        def body(r, carry):
            fn(pl.ds(pl.multiple_of(r * FFN_ROW_CHUNK, FFN_ROW_CHUNK), FFN_ROW_CHUNK))
            return carry
        lax.fori_loop(0, tm // FFN_ROW_CHUNK, body, 0)

    @pl.when(j == 0)
    def _():
        def prologue(rows):
            x = x_ref[rows, :]
            hn_ref[rows, :] = (x * _rms_scale(x) * gpre_ref[...]).astype(BF16)
            o_ref[rows, :] = jnp.zeros((FFN_ROW_CHUNK, o_ref.shape[1]), F32)
        row_chunks(prologue)

    w_gu = jnp.concatenate([ga_ref[...], gb_ref[...], ua_ref[...], ub_ref[...]], axis=1).astype(BF16)
    wb = wb_ref[...]
    if odd_tail:
        wb = jnp.where(j == last, 0.0, wb)
    w_o = jnp.concatenate([wa_ref[...], wb], axis=0).astype(BF16)
    gu = _dot(hn_ref[...], w_gu)
    half = gu.shape[1] // 2
    act = (_silu(gu[:, :half]) * gu[:, half:]).astype(BF16)
    o_ref[...] += _dot(act, w_o)

    @pl.when(j == last)
    def _():
        def epilogue(rows):
            o_ref[rows, :] = _postnorm_residual(x_ref[rows, :], o_ref[rows, :], gpost_ref[...], 0.5)
        row_chunks(epilogue)


def ffn(x, g_pre, g_post, w_in, w_out, lead, *, tm):
    m, d = x.shape
    ff = w_out.shape[-2]
    blk = V7X_LANES
    nblk = ff // blk
    assert nblk * blk == ff and w_in.shape[-2:] == (d, 2 * ff)
    nf = -(-nblk // 2)
    second = lambda j: jnp.minimum(2 * j + 1, nblk - 1)
    col = lambda f: _wspec(lead, (d, blk), lambda i, j: (0, f(j)))
    row = lambda f: _wspec(lead, (blk, d), lambda i, j: (f(j), 0))
    vec = pl.BlockSpec((1, d), lambda i, j: (0, 0))
    return pl.pallas_call(
        functools.partial(_ffn_body, odd_tail=bool(nblk % 2)),
        grid=(m // tm, nf),
        in_specs=[
            pl.BlockSpec((tm, d), lambda i, j: (i, 0)),
            vec,
            col(lambda j: 2 * j), col(second),
            col(lambda j: nblk + 2 * j), col(lambda j: nblk + second(j)),
            row(lambda j: 2 * j), row(second),
            vec,
        ],
        out_specs=pl.BlockSpec((tm, d), lambda i, j: (i, 0)),
        out_shape=jax.ShapeDtypeStruct((m, d), F32),
        scratch_shapes=[pltpu.VMEM((tm, d), BF16)],
        compiler_params=_cparams(("parallel", "arbitrary")),
        name="ffn",
    )(x, g_pre.reshape(1, d), w_in, w_in, w_in, w_in, w_out, w_out, g_post.reshape(1, d))


def _mixer_out_body(o_ref, om_ref, w1_ref, w2_ref, x_ref, g_ref, out_ref):
    y = _dot(o_ref[...], w1_ref[...]) + _dot(om_ref[...], w2_ref[...])
    out_ref[...] = _postnorm_residual(x_ref[...], y, g_ref[...], 1.0)


def mixer_out_proj(o, o_mem, w_out, x, gain, *, tm):
    m, d = x.shape
    k1, k2 = o.shape[1], o_mem.shape[1]
    assert k1 % k2 == 0 and w_out.shape == (k1 + k2, d)
    return pl.pallas_call(
        _mixer_out_body,
        grid=(m // tm,),
        in_specs=[
            pl.BlockSpec((tm, k1), lambda i: (i, 0)),
            pl.BlockSpec((tm, k2), lambda i: (i, 0)),
            pl.BlockSpec((k1, d), lambda i: (0, 0)),
            pl.BlockSpec((k2, d), lambda i: (k1 // k2, 0)),
            pl.BlockSpec((tm, d), lambda i: (i, 0)),
            pl.BlockSpec((1, d), lambda i: (0, 0)),
        ],
        out_specs=pl.BlockSpec((tm, d), lambda i: (i, 0)),
        out_shape=jax.ShapeDtypeStruct((m, d), F32),
        compiler_params=_cparams(("parallel",)),
        name="mixer_out_proj",
    )(o, o_mem, w_out, w_out, x, gain.reshape(1, d))


def _mem_attn_body(q_ref, k_ref, v_ref, o_ref):
    for h in range(MEM_HEADS):
        cs = slice(h * HEAD_DIM, (h + 1) * HEAD_DIM)
        s = _dot_nt(q_ref[:, cs], k_ref[:, cs]) * (HEAD_DIM ** -0.5)
        e = jnp.exp(s - jnp.max(s, axis=-1, keepdims=True))
        den = jnp.sum(e, axis=-1, keepdims=True)
        o_ref[:, cs] = (_dot(e.astype(BF16), v_ref[:, cs]) / den).astype(o_ref.dtype)


def mem_attention(proj, q_col_block, mem_kv, *, tm):
    m = proj.shape[0]
    mlen = mem_kv.shape[0]
    return pl.pallas_call(
        _mem_attn_body,
        grid=(m // tm,),
        in_specs=[
            pl.BlockSpec((tm, MEM_WIDTH), lambda i: (i, q_col_block)),
            pl.BlockSpec((mlen, MEM_WIDTH), lambda i: (0, 0)),
            pl.BlockSpec((mlen, MEM_WIDTH), lambda i: (0, 1)),
        ],
        out_specs=pl.BlockSpec((tm, MEM_WIDTH), lambda i: (i, 0)),
        out_shape=jax.ShapeDtypeStruct((m, MEM_WIDTH), BF16),
        compiler_params=_cparams(("parallel",)),
        name="mem_attention",
    )(proj, mem_kv, mem_kv)


def _gate_body(ab_ref, alog_ref, dtb_ref, gc_ref, beta_ref, gdk_ref):
    tl = ab_ref.shape[1]
    a = ab_ref[0:GATE_ROWS, :]
    b = ab_ref[GATE_ROWS:2 * GATE_ROWS, :]
    z = a + dtb_ref[...]
    softplus = jnp.maximum(z, 0.0) + jnp.log1p(jnp.exp(-jnp.abs(z)))
    g = -jnp.exp(alog_ref[...]) * softplus
    beta_ref[...] = jax.nn.sigmoid(b)
    r = lax.broadcasted_iota(jnp.int32, (V7X_LANES, V7X_LANES), 0)
    c = lax.broadcasted_iota(jnp.int32, (V7X_LANES, V7X_LANES), 1)
    same = _block_id(r, CHUNK) == _block_id(c, CHUNK)
    upper = jnp.where(same & (r <= c), 1.0, 0.0).astype(F32)
    ones = jnp.where(same, 1.0, 0.0).astype(F32)
    for s in range(tl // V7X_LANES):
        ls = slice(s * V7X_LANES, (s + 1) * V7X_LANES)
        gs = g[:, ls]
        gc = jnp.dot(gs, upper, preferred_element_type=F32, precision=lax.Precision.HIGHEST)
        gend = jnp.dot(gs, ones, preferred_element_type=F32, precision=lax.Precision.HIGHEST)
        gc_ref[:, ls] = gc
        gdk_ref[:, ls] = gend - gc


def dn_gates(ab_t, a_log, dt_bias, *, tl):
    r, s = ab_t.shape
    pad = lambda v: jnp.pad(v.astype(F32), (0, GATE_ROWS - v.shape[0])).reshape(GATE_ROWS, 1)
    out = jax.ShapeDtypeStruct((GATE_ROWS, s), F32)
    return pl.pallas_call(
        _gate_body,
        grid=(s // tl,),
        in_specs=[
            pl.BlockSpec((r, tl), lambda i: (0, i)),
            pl.BlockSpec((GATE_ROWS, 1), lambda i: (0, 0)),
            pl.BlockSpec((GATE_ROWS, 1), lambda i: (0, 0)),
        ],
        out_specs=[pl.BlockSpec((GATE_ROWS, tl), lambda i: (0, i))] * 3,
        out_shape=[out, out, out],
        compiler_params=_cparams(("parallel",)),
        name="dn_gates",
    )(ab_t, pad(a_log), pad(dt_bias))


def _col_bcast(row):
    n = row.shape[1]
    return jnp.broadcast_to(row, (n, n)).T


def _unit_lower_inverses(nmats, ri, ci):
    def quarter(b):
        same_block = _block_id(ri, 2 * b) == _block_id(ci, 2 * b)
        return same_block & ((ri & (2 * b - 1)) >= b) & ((ci & (2 * b - 1)) < b)

    eye = jnp.where(ri == ci, 1.0, 0.0).astype(F32)
    first = quarter(1)
    xs = [eye - jnp.where(first, n, 0.0) for n in nmats]
    b = 2
    while b < CHUNK:
        mask = quarter(b)
        ys = [_dot(jnp.where(mask, n, 0.0).astype(BF16), x.astype(BF16)) for n, x in zip(nmats, xs)]
        xs = [x - _dot(x.astype(BF16), y.astype(BF16)) for x, y in zip(xs, ys)]
        b *= 2
    return xs


def _delta_body(q_ref, k_ref, v_ref, z_ref, qh_ref, kh_ref, vh_ref, wq_ref, wk_ref, wv_ref,
                gc_ref, beta_ref, gdk_ref, on_ref, o_ref, state_ref, buf_ref, *, tc, hb):
    t = pl.program_id(1)
    halo = V7X_BF16_SUBLANES
    n_super = tc // SUPER
    n_chunk = SUPER // CHUNK

    @pl.when(t == 0)
    def _():
        state_ref[...] = jnp.zeros_like(state_ref)

    states = [state_ref[h] for h in range(hb)]

    ri = lax.broadcasted_iota(jnp.int32, (SUPER, SUPER), 0)
    ci = lax.broadcasted_iota(jnp.int32, (SUPER, SUPER), 1)
    same = _block_id(ri, CHUNK) == _block_id(ci, CHUNK)
    causal = same & (ri >= ci)
    strict = same & (ri > ci)

    def conv_silu(slot, cur_ref, halo_ref, w_ref, cs):
        buf_ref[slot, 0:halo, :] = jnp.where(t == 0, 0.0, halo_ref[:, cs].astype(F32))
        buf_ref[slot, halo:halo + tc, :] = cur_ref[:, cs].astype(F32)
        y = jnp.zeros((tc, HEAD_DIM), F32)
        for j in range(CONV_K):
            off = halo - (CONV_K - 1) + j
            y = y + w_ref[j:j + 1, cs] * buf_ref[slot, off:off + tc, :]
        return _silu(y)

    def l2n(x):
        return x * lax.rsqrt(jnp.sum(x * x, axis=-1, keepdims=True) + NORM_EPS)

    heads = range(hb)
    lanes = [slice(h * HEAD_DIM, (h + 1) * HEAD_DIM) for h in heads]
    q_all = [l2n(conv_silu(3 * h, q_ref, qh_ref, wq_ref, lanes[h])) * (HEAD_DIM ** -0.5) for h in heads]
    k_all = [l2n(conv_silu(3 * h + 1, k_ref, kh_ref, wk_ref, lanes[h])) for h in heads]
    v_all = [conv_silu(3 * h + 2, v_ref, vh_ref, wv_ref, lanes[h]) for h in heads]

    units = [(h, s) for s in range(n_super) for h in heads]
    rows = lambda s: slice(s * SUPER, (s + 1) * SUPER)
    q = [q_all[h][rows(s)] for h, s in units]
    k = [k_all[h][rows(s)] for h, s in units]
    v = [v_all[h][rows(s)] for h, s in units]
    g_row = [gc_ref[h, :, rows(s)] for h, s in units]
    gdk_row = [gdk_ref[h, :, rows(s)] for h, s in units]
    g_col = [_col_bcast(g) for g in g_row]
    gdk_col = [_col_bcast(g) for g in gdk_row]
    beta_col = [_col_bcast(beta_ref[h, :, rows(s)]) for h, s in units]
    decay = [jnp.exp(jnp.where(causal, gc - jnp.broadcast_to(gr, (SUPER, SUPER)), -jnp.inf))
             for gc, gr in zip(g_col, g_row)]
    exp_g = [jnp.exp(g) for g in g_col]
    g_last = [jnp.broadcast_to(jnp.exp(gr + gd), (SUPER, SUPER)) for gr, gd in zip(g_row, gdk_row)]

    kb = [x.astype(BF16) for x in k]
    kk = [_dot_nt(x, x) for x in kb]
    qk = [_dot_nt(a.astype(BF16), b) for a, b in zip(q, kb)]
    nmat = [jnp.where(strict, b * x * d, 0.0) for b, x, d in zip(beta_col, kk, decay)]
    tinv = _unit_lower_inverses(nmat, ri, ci)
    rhs = [jnp.concatenate([kx * b * e, vx * b], axis=1).astype(BF16)
           for kx, vx, b, e in zip(k, v, beta_col, exp_g)]
    wu = [_dot(x.astype(BF16), r).astype(BF16) for x, r in zip(tinv, rhs)]
    attn = [(a * d).astype(BF16) for a, d in zip(qk, decay)]
    a_wu = [_dot(a, x) for a, x in zip(attn, wu)]
    q_eff = [(qx * e - aw[:, :HEAD_DIM]).astype(BF16) for qx, e, aw in zip(q, exp_g, a_wu)]
    o_intra = [aw[:, HEAD_DIM:] for aw in a_wu]
    kd_t = [(kx * jnp.exp(g)).T for kx, g in zip(k, gdk_col)]
    in_chunk = [_block_id(ci, CHUNK) == c for c in range(n_chunk)]
    pr = [[_dot(jnp.where(in_chunk[c], kt, 0.0).astype(BF16), x) for c in range(n_chunk)]
          for kt, x in zip(kd_t, wu)]

    outs = [[] for _ in heads]
    for s in range(n_super):
        for c in range(n_chunk):
            cr = slice(c * CHUNK, (c + 1) * CHUNK)
            for h in heads:
                u = s * hb + h
                lhs = jnp.concatenate([q_eff[u][cr], pr[u][c][:, :HEAD_DIM].astype(BF16)], axis=0)
                r = _dot(lhs, states[h].astype(BF16))
                outs[h].append(r[:CHUNK] + o_intra[u][cr])
                gl = jnp.broadcast_to(g_last[u][:, c * CHUNK:c * CHUNK + 1], (HEAD_DIM, HEAD_DIM))
                states[h] = states[h] * gl + (pr[u][c][:, HEAD_DIM:] - r[CHUNK:])

    o_norm = on_ref[...]
    for h in heads:
        o = jnp.concatenate(outs[h], axis=0)
        o = o * _rms_scale(o) * o_norm * _silu(z_ref[:, lanes[h]].astype(F32))
        o_ref[:, lanes[h]] = o.astype(o_ref.dtype)
    for h in heads:
        state_ref[h] = states[h]


def delta_core(proj, conv_w, gc, beta, gdk, o_norm, *, tc, hb):
    s = proj.shape[0]
    nh = DN_HEADS
    assert nh % hb == 0
    ng = nh // hb
    halo = V7X_BF16_SUBLANES
    nhalo = tc // halo
    width = hb * HEAD_DIM

    def cur(sec):
        return pl.BlockSpec((tc, width), lambda g, t: (t, sec * ng + g))

    def prev(sec):
        return pl.BlockSpec((halo, width), lambda g, t: (jnp.maximum(t * nhalo - 1, 0), sec * ng + g))

    def cw(sec):
        return pl.BlockSpec((CONV_K, width), lambda g, t: (0, sec * ng + g))

    gate = pl.BlockSpec((hb, 1, tc), lambda g, t: (g, 0, t))
    g3 = lambda a: a.reshape(GATE_ROWS, 1, s)
    return pl.pallas_call(
        functools.partial(_delta_body, tc=tc, hb=hb),
        grid=(ng, s // tc),
        in_specs=[cur(0), cur(1), cur(2), cur(3), prev(0), prev(1), prev(2),
                  cw(0), cw(1), cw(2), gate, gate, gate,
                  pl.BlockSpec((1, HEAD_DIM), lambda g, t: (0, 0))],
        out_specs=pl.BlockSpec((tc, width), lambda g, t: (t, g)),
        out_shape=jax.ShapeDtypeStruct((s, nh * HEAD_DIM), BF16),
        scratch_shapes=[pltpu.VMEM((hb, HEAD_DIM, HEAD_DIM), F32),
                        pltpu.VMEM((3 * hb, tc + halo, HEAD_DIM), F32)],
        compiler_params=_cparams(("parallel", "arbitrary")),
        name="delta_core",
    )(proj, proj, proj, proj, proj, proj, proj, conv_w, conv_w, conv_w,
      g3(gc), g3(beta), g3(gdk), o_norm.reshape(1, HEAD_DIM))


def _dilated_body(*refs, dils):
    ng = len(dils)
    q_refs, kc_refs, kp_refs, vc_refs, vp_refs, slope_refs = (refs[i * ng:(i + 1) * ng] for i in range(6))
    o_ref, qf, kf, vf, of, lf = refs[6 * ng:]
    n = pl.program_id(0)
    blk = DIL_BLOCK
    tt = o_ref.shape[0]
    ri = lax.broadcasted_iota(jnp.int32, (blk, 2 * blk), 0)
    ci = lax.broadcasted_iota(jnp.int32, (blk, 2 * blk), 1)
    rel = ri + blk - ci
    window = (rel >= 0) & (rel <= blk)
    window_first = window & ((ci >= blk) | (n > 0))
    rel_f = rel.astype(F32)

    for g, dil in enumerate(dils):
        span = blk * dil
        qf[...] = q_refs[g][...].astype(F32)
        kf[0:span, :] = kp_refs[g][...].astype(F32)
        kf[span:span + tt, :] = kc_refs[g][...].astype(F32)
        vf[0:span, :] = vp_refs[g][...].astype(F32)
        vf[span:span + tt, :] = vc_refs[g][...].astype(F32)
        bias = (slope_refs[g][:, 0:1] * (-float(dil))) * rel_f

        def rows(start, size, dil=dil):
            return pl.ds(start, size, stride=dil) if dil > 1 else pl.ds(start, size)

        units = [(b, c + span * b) for b in range(tt // span) for c in range(dil)]
        qs = [qf[rows(st, blk), :].astype(BF16) for _, st in units]
        ks = [kf[rows(st, 2 * blk), :].astype(BF16) for _, st in units]
        vs = [vf[rows(st, 2 * blk), :].astype(BF16) for _, st in units]
        s = [_dot_nt(a, b_) * (HEAD_DIM ** -0.5) + bias for a, b_ in zip(qs, ks)]
        s = [jnp.where(window_first if b == 0 else window, x, -jnp.inf) for (b, _), x in zip(units, s)]
        mx = [jnp.max(x, axis=-1, keepdims=True) for x in s]
        e = [jnp.exp(x - m) for x, m in zip(s, mx)]
        den = [jnp.sum(x, axis=-1, keepdims=True) for x in e]
        o = [_dot(x.astype(BF16), v) / d for x, v, d in zip(e, vs, den)]
        for (_, st), ou, m, d in zip(units, o, mx, den):
            of[g, rows(st, blk), :] = ou
            lf[g, rows(st, blk), :] = jnp.broadcast_to(m + jnp.log(d), (blk, HEAD_DIM))

    lses = [lf[g] for g in range(ng)]
    top = functools.reduce(jnp.maximum, lses)
    wts = [jnp.exp(x - top) for x in lses]
    num = sum(w * of[g] for g, w in enumerate(wts))
    o_ref[...] = (num / sum(wts)).astype(o_ref.dtype)


def dilated_attention(qproj, kv, *, tt):
    s = qproj.shape[0]
    dils = tuple(dil for _, dil in DIL_GROUPS)
    assert all(win // dil == DIL_BLOCK for win, dil in DIL_GROUPS)
    assert all(tt % (DIL_BLOCK * dil) == 0 for dil in dils) and s % tt == 0
    hpg = DIL_HEADS_PER_GROUP
    slopes = jnp.asarray([2.0 ** (-8.0 * (h + 1) / DIL_HEADS) for h in range(DIL_HEADS)], F32)
    slopes = jnp.broadcast_to(slopes.reshape(len(dils), hpg, 1, 1), (len(dils), hpg, 1, HEAD_DIM))

    def cur(col0):
        return [pl.BlockSpec((tt, HEAD_DIM), lambda n, h, g=g: (n, col0 + g * hpg + h)) for g in range(len(dils))]

    def prev(col0):
        specs = []
        for g, dil in enumerate(dils):
            span = DIL_BLOCK * dil
            specs.append(pl.BlockSpec(
                (span, HEAD_DIM),
                lambda n, h, g=g, per=tt // span: (jnp.maximum(n * per - 1, 0), col0 + g * hpg + h)))
        return specs

    slope_specs = [pl.BlockSpec((None, None, 1, HEAD_DIM), lambda n, h, g=g: (g, h, 0, 0)) for g in range(len(dils))]
    ng = len(dils)
    max_span = DIL_BLOCK * max(dils)
    return pl.pallas_call(
        functools.partial(_dilated_body, dils=dils),
        grid=(s // tt, hpg),
        in_specs=cur(0) + cur(0) + prev(0) + cur(DIL_HEADS) + prev(DIL_HEADS) + slope_specs,
        out_specs=pl.BlockSpec((tt, HEAD_DIM), lambda n, h: (n, h)),
        out_shape=jax.ShapeDtypeStruct((s, hpg * HEAD_DIM), BF16),
        scratch_shapes=[pltpu.VMEM((tt, HEAD_DIM), F32),
                        pltpu.VMEM((max_span + tt, HEAD_DIM), F32),
                        pltpu.VMEM((max_span + tt, HEAD_DIM), F32),
                        pltpu.VMEM((ng, tt, HEAD_DIM), F32),
                        pltpu.VMEM((ng, tt, HEAD_DIM), F32)],
        compiler_params=_cparams(("parallel", "arbitrary")),
        name="dilated_attention",
    )(*([qproj] * ng + [kv] * (4 * ng) + [slopes] * ng))


def _deltanet_layer(x, g_pre, g_post, mem_kv, dn_w_in, l, conv_w, a_log, dt_bias, o_norm, dn_w_out):
    qkvz = 4 * DN_WIDTH
    w_in = dn_w_in[l]
    w_mq = w_in[:, qkvz + 2 * DN_HEADS:]
    gate_w = lambda lo: jnp.pad(w_in[:, lo:lo + DN_HEADS].T, ((0, GATE_ROWS - DN_HEADS), (0, 0)))
    w_ab_t = jnp.concatenate([gate_w(qkvz), gate_w(qkvz + DN_HEADS)], axis=0).astype(BF16)
    proj, ab_t = dn_in_proj(x, g_pre, dn_w_in, (l,), w_mq, w_ab_t, tm=1024, tn=MEM_WIDTH)
    gc, beta, gdk = dn_gates(ab_t, a_log, dt_bias, tl=1024)
    o = delta_core(proj, conv_w.astype(F32), gc, beta, gdk, o_norm, tc=256, hb=12)
    o_mem = mem_attention(proj, qkvz // MEM_WIDTH, mem_kv, tm=512)
    return mixer_out_proj(o, o_mem, dn_w_out[l].astype(BF16), x, g_post, tm=512)


def _dilated_layer(x, g_pre, g_post, kv, mem_kv, dil_w_in, i, dil_w_out):
    proj = prenorm_matmul(x, g_pre, dil_w_in, (i,), tm=1024, tn=512)
    o = dilated_attention(proj, kv, tt=2048)
    o_mem = mem_attention(proj, DIL_WIDTH // MEM_WIDTH, mem_kv, tm=512)
    return mixer_out_proj(o, o_mem, dil_w_out[i].astype(BF16), x, g_post, tm=512)


def kernel(x, mem, norm_gains, ffn_w_in, ffn_w_out, mem_norm_gain, w_mem_kv, dn_w_in, dn_conv, dn_a_log,
           dn_dt_bias, dn_o_norm, dn_w_out, kv_norm_gain, w_kv, dil_w_in, dil_w_out):
    batch, seq, d = x.shape
    assert batch == 1
    depth = norm_gains.shape[0]
    n_a = dn_w_in.shape[0]
    xs = x.reshape(seq, d)
    mem2 = mem.reshape(mem.shape[1], d)
    kv = None
    for l in range(depth):
        gains = norm_gains[l]
        if l == n_a:
            kv = prenorm_matmul(xs, kv_norm_gain, w_kv, tm=1024, tn=512)
        xs = ffn(xs, gains[0], gains[1], ffn_w_in, ffn_w_out, (l, 0), tm=1024)
        mem_kv = prenorm_matmul(mem2, mem_norm_gain[l], w_mem_kv, (l,), tm=mem2.shape[0], tn=512)
        if l < n_a:
            xs = _deltanet_layer(xs, gains[2], gains[3], mem_kv, dn_w_in, l, dn_conv[l], dn_a_log[l],
                                 dn_dt_bias[l], dn_o_norm[l], dn_w_out)
        else:
            i = l - n_a
            xs = _dilated_layer(xs, gains[2], gains[3], kv, mem_kv, dil_w_in, i, dil_w_out)
        xs = ffn(xs, gains[4], gains[5], ffn_w_in, ffn_w_out, (l, 1), tm=1024)
    return xs.reshape(batch, seq, d)
```

```python
import functools
import math

import jax
import jax.numpy as jnp
from jax import lax
from jax.experimental import pallas as pl
from jax.experimental.pallas import tpu as pltpu

F32 = jnp.float32
BF16 = jnp.bfloat16

D_MODEL = 2048
HEAD_DIM = 128
DN_HEADS = 12
DN_WIDTH = DN_HEADS * HEAD_DIM
CONV_K = 4
CHUNK = 64
MEM_HEADS = 4
MEM_WIDTH = MEM_HEADS * HEAD_DIM
DIL_GROUPS = ((128, 1), (512, 4), (2048, 16))
DIL_HEADS_PER_GROUP = 4
DIL_HEADS = DIL_HEADS_PER_GROUP * len(DIL_GROUPS)
DIL_WIDTH = DIL_HEADS * HEAD_DIM
DIL_OUT = DIL_HEADS_PER_GROUP * HEAD_DIM
DIL_BLOCK = 128
D_FF = ((8 * D_MODEL // 3 + 127) // 128) * 128
NORM_EPS = 1e-6

V7X_LANES = 128
V7X_BF16_SUBLANES = 16
V7X_VMEM_LIMIT_BYTES = 56 * 1024 * 1024

FF_TILE = 512
D_FF_PAD = -(-D_FF // FF_TILE) * FF_TILE
GATE_ROWS = 16
SUPER = 2 * CHUNK


def _cparams(sem):
    return pltpu.CompilerParams(dimension_semantics=sem, vmem_limit_bytes=V7X_VMEM_LIMIT_BYTES)


def _rms_scale(x):
    return lax.rsqrt(jnp.mean(x * x, axis=-1, keepdims=True) + NORM_EPS)


def _silu(x):
    return x * jax.nn.sigmoid(x)


def _block_id(idx, size):
    assert size & (size - 1) == 0
    return lax.shift_right_logical(idx, size.bit_length() - 1)


def _dot(a, b):
    return jnp.dot(a, b, preferred_element_type=F32)


def _dot_nt(a, b):
    return lax.dot_general(a, b, (((1,), (1,)), ((), ())), preferred_element_type=F32)


def _prenorm_matmul_body(x_ref, g_ref, w_ref, o_ref, hn_ref):
    @pl.when(pl.program_id(1) == 0)
    def _():
        x = x_ref[...]
        hn_ref[...] = (x * _rms_scale(x) * g_ref[...]).astype(BF16)

    o_ref[...] = _dot(hn_ref[...], w_ref[...].astype(BF16)).astype(o_ref.dtype)


def _wspec(lead, block, index_fn):
    return pl.BlockSpec((None,) * len(lead) + block, lambda *g: lead + index_fn(*g))


def prenorm_matmul(x, gain, w, lead=(), *, tm, tn, out_dtype=BF16):
    m, d = x.shape
    n = w.shape[-1]
    return pl.pallas_call(
        _prenorm_matmul_body,
        grid=(m // tm, n // tn),
        in_specs=[
            pl.BlockSpec((tm, d), lambda i, j: (i, 0)),
            pl.BlockSpec((1, d), lambda i, j: (0, 0)),
            _wspec(lead, (d, tn), lambda i, j: (0, j)),
        ],
        out_specs=pl.BlockSpec((tm, tn), lambda i, j: (i, j)),
        out_shape=jax.ShapeDtypeStruct((m, n), out_dtype),
        scratch_shapes=[pltpu.VMEM((tm, d), BF16)],
        compiler_params=_cparams(("parallel", "arbitrary")),
        name="prenorm_matmul",
    )(x, gain.reshape(1, d), w)


def _dn_in_proj_body(x_ref, g_ref, w_ref, wmq_ref, wab_ref, o_ref, mq_ref, ab_ref, hn_ref):
    @pl.when(pl.program_id(1) == 0)
    def _():
        x = x_ref[...]
        hn = (x * _rms_scale(x) * g_ref[...]).astype(BF16)
        hn_ref[...] = hn
        ab_ref[...] = _dot_nt(wab_ref[...], hn)
        mq_ref[...] = _dot_nt(hn, wmq_ref[...]).astype(mq_ref.dtype)

    o_ref[...] = _dot_nt(hn_ref[...], w_ref[...].astype(BF16)).astype(o_ref.dtype)


def dn_in_proj(x, gain, w_in_t, lead, w_mq_t, w_ab_t, *, tm, tn):
    m, d = x.shape
    n_main = (4 * DN_WIDTH) // tn
    assert n_main * tn == 4 * DN_WIDTH
    nq = w_mq_t.shape[0]
    r = w_ab_t.shape[0]
    return pl.pallas_call(
        _dn_in_proj_body,
        grid=(m // tm, n_main),
        in_specs=[
            pl.BlockSpec((tm, d), lambda i, j: (i, 0)),
            pl.BlockSpec((1, d), lambda i, j: (0, 0)),
            _wspec(lead, (tn, d), lambda i, j: (j, 0)),
            pl.BlockSpec((nq, d), lambda i, j: (0, 0)),
            pl.BlockSpec((r, d), lambda i, j: (0, 0)),
        ],
        out_specs=[
            pl.BlockSpec((tm, tn), lambda i, j: (i, j)),
            pl.BlockSpec((tm, nq), lambda i, j: (i, 0)),
            pl.BlockSpec((r, tm), lambda i, j: (0, i)),
        ],
        out_shape=[
            jax.ShapeDtypeStruct((m, n_main * tn), BF16),
            jax.ShapeDtypeStruct((m, nq), BF16),
            jax.ShapeDtypeStruct((r, m), F32),
        ],
        scratch_shapes=[pltpu.VMEM((tm, d), BF16)],
        compiler_params=_cparams(("parallel", "arbitrary")),
        name="dn_in_proj",
    )(x, gain.reshape(1, d), w_in_t, w_mq_t, w_ab_t)


def _postnorm_residual(x, y, gain, scale):
    return x + scale * (y * _rms_scale(y) * gain)


FFN_ROW_GROUP = 256


def _ffn_body(x_ref, gpre_ref, ga_ref, gb_ref, ua_ref, ub_ref, wa_ref, wb_ref, gpost_ref, o_ref, hn_ref,
              *, odd_tail):
    j = pl.program_id(1)
    last = pl.num_programs(1) - 1
    tm = x_ref.shape[0]
    groups = [slice(r, r + FFN_ROW_GROUP) for r in range(0, tm, FFN_ROW_GROUP)]

    def weights(zero_tail):
        w_gu = jnp.concatenate([ga_ref[...], gb_ref[...], ua_ref[...], ub_ref[...]], axis=1).astype(BF16)
        wb = jnp.zeros(wb_ref.shape, wb_ref.dtype) if zero_tail else wb_ref[...]
        return w_gu, jnp.concatenate([wa_ref[...], wb], axis=0).astype(BF16)

    def down(hn, w_gu, w_o):
        gu = _dot(hn, w_gu)
        half = gu.shape[1] // 2
        return _dot((_silu(gu[:, :half]) * gu[:, half:]).astype(BF16), w_o)

    @pl.when(j == 0)
    def _():
        w = weights(False)
        for rows in groups:
            x = x_ref[rows, :]
            hn = (x * _rms_scale(x) * gpre_ref[...]).astype(BF16)
            hn_ref[rows, :] = hn
            o_ref[rows, :] = down(hn, *w)

    @pl.when((j > 0) & (j < last))
    def _():
        o_ref[...] += down(hn_ref[...], *weights(False))

    @pl.when(j == last)
    def _():
        w = weights(odd_tail)
        for rows in groups:
            y = o_ref[rows, :] + down(hn_ref[rows, :], *w)
            o_ref[rows, :] = _postnorm_residual(x_ref[rows, :], y, gpost_ref[...], 0.5)


def ffn(x, g_pre, g_post, w_in, w_out, lead, *, tm):
    m, d = x.shape
    ff = w_out.shape[-2]
    blk = V7X_LANES
    nblk = ff // blk
    assert nblk * blk == ff and w_in.shape[-2:] == (d, 2 * ff)
    nf = -(-nblk // 2)
    assert nf >= 2 and tm % FFN_ROW_GROUP == 0
    second = lambda j: jnp.minimum(2 * j + 1, nblk - 1)
    col = lambda f: _wspec(lead, (d, blk), lambda i, j: (0, f(j)))
    row = lambda f: _wspec(lead, (blk, d), lambda i, j: (f(j), 0))
    vec = pl.BlockSpec((1, d), lambda i, j: (0, 0))
    return pl.pallas_call(
        functools.partial(_ffn_body, odd_tail=bool(nblk % 2)),
        grid=(m // tm, nf),
        in_specs=[
            pl.BlockSpec((tm, d), lambda i, j: (i, 0)),
            vec,
            col(lambda j: 2 * j), col(second),
            col(lambda j: nblk + 2 * j), col(lambda j: nblk + second(j)),
            row(lambda j: 2 * j), row(second),
            vec,
        ],
        out_specs=pl.BlockSpec((tm, d), lambda i, j: (i, 0)),
        out_shape=jax.ShapeDtypeStruct((m, d), F32),
        scratch_shapes=[pltpu.VMEM((tm, d), BF16)],
        compiler_params=_cparams(("parallel", "arbitrary")),
        name="ffn",
    )(x, g_pre.reshape(1, d), w_in, w_in, w_in, w_in, w_out, w_out, g_post.reshape(1, d))


def _mixer_out_body(o_ref, om_ref, w1_ref, w2_ref, x_ref, g_ref, out_ref):
    y = _dot(o_ref[...], w1_ref[...]) + _dot(om_ref[...], w2_ref[...])
    out_ref[...] = _postnorm_residual(x_ref[...], y, g_ref[...], 1.0)


def mixer_out_proj(o, o_mem, w_out, x, gain, *, tm):
    m, d = x.shape
    k1, k2 = o.shape[1], o_mem.shape[1]
    assert k1 % k2 == 0 and w_out.shape == (k1 + k2, d)
    return pl.pallas_call(
        _mixer_out_body,
        grid=(m // tm,),
        in_specs=[
            pl.BlockSpec((tm, k1), lambda i: (i, 0)),
            pl.BlockSpec((tm, k2), lambda i: (i, 0)),
            pl.BlockSpec((k1, d), lambda i: (0, 0)),
            pl.BlockSpec((k2, d), lambda i: (k1 // k2, 0)),
            pl.BlockSpec((tm, d), lambda i: (i, 0)),
            pl.BlockSpec((1, d), lambda i: (0, 0)),
        ],
        out_specs=pl.BlockSpec((tm, d), lambda i: (i, 0)),
        out_shape=jax.ShapeDtypeStruct((m, d), F32),
        compiler_params=_cparams(("parallel",)),
        name="mixer_out_proj",
    )(o, o_mem, w_out, w_out, x, gain.reshape(1, d))


def _mem_attn_body(q_ref, k_ref, v_ref, o_ref):
    for h in range(MEM_HEADS):
        cs = slice(h * HEAD_DIM, (h + 1) * HEAD_DIM)
        s = _dot_nt(q_ref[:, cs], k_ref[:, cs]) * (HEAD_DIM ** -0.5)
        e = jnp.exp(s - jnp.max(s, axis=-1, keepdims=True))
        den = jnp.sum(e, axis=-1, keepdims=True)
        o_ref[:, cs] = (_dot(e.astype(BF16), v_ref[:, cs]) / den).astype(o_ref.dtype)


def mem_attention(proj, q_col_block, mem_kv, *, tm):
    m = proj.shape[0]
    mlen = mem_kv.shape[0]
    return pl.pallas_call(
        _mem_attn_body,
        grid=(m // tm,),
        in_specs=[
            pl.BlockSpec((tm, MEM_WIDTH), lambda i: (i, q_col_block)),
            pl.BlockSpec((mlen, MEM_WIDTH), lambda i: (0, 0)),
            pl.BlockSpec((mlen, MEM_WIDTH), lambda i: (0, 1)),
        ],
        out_specs=pl.BlockSpec((tm, MEM_WIDTH), lambda i: (i, 0)),
        out_shape=jax.ShapeDtypeStruct((m, MEM_WIDTH), BF16),
        compiler_params=_cparams(("parallel",)),
        name="mem_attention",
    )(proj, mem_kv, mem_kv)


def _gate_body(ab_ref, alog_ref, dtb_ref, gc_ref, beta_ref, gdk_ref):
    tl = ab_ref.shape[1]
    a = ab_ref[0:GATE_ROWS, :]
    b = ab_ref[GATE_ROWS:2 * GATE_ROWS, :]
    z = a + dtb_ref[...]
    softplus = jnp.maximum(z, 0.0) + jnp.log1p(jnp.exp(-jnp.abs(z)))
    g = -jnp.exp(alog_ref[...]) * softplus
    beta_ref[...] = jax.nn.sigmoid(b)
    r = lax.broadcasted_iota(jnp.int32, (V7X_LANES, V7X_LANES), 0)
    c = lax.broadcasted_iota(jnp.int32, (V7X_LANES, V7X_LANES), 1)
    same = _block_id(r, CHUNK) == _block_id(c, CHUNK)
    upper = jnp.where(same & (r <= c), 1.0, 0.0).astype(F32)
    ones = jnp.where(same, 1.0, 0.0).astype(F32)
    for s in range(tl // V7X_LANES):
        ls = slice(s * V7X_LANES, (s + 1) * V7X_LANES)
        gs = g[:, ls]
        gc = jnp.dot(gs, upper, preferred_element_type=F32, precision=lax.Precision.HIGHEST)
        gend = jnp.dot(gs, ones, preferred_element_type=F32, precision=lax.Precision.HIGHEST)
        gc_ref[:, ls] = gc
        gdk_ref[:, ls] = gend - gc


def dn_gates(ab_t, a_log, dt_bias, *, tl):
    r, s = ab_t.shape
    pad = lambda v: jnp.pad(v.astype(F32), (0, GATE_ROWS - v.shape[0])).reshape(GATE_ROWS, 1)
    out = jax.ShapeDtypeStruct((GATE_ROWS, s), F32)
    return pl.pallas_call(
        _gate_body,
        grid=(s // tl,),
        in_specs=[
            pl.BlockSpec((r, tl), lambda i: (0, i)),
            pl.BlockSpec((GATE_ROWS, 1), lambda i: (0, 0)),
            pl.BlockSpec((GATE_ROWS, 1), lambda i: (0, 0)),
        ],
        out_specs=[pl.BlockSpec((GATE_ROWS, tl), lambda i: (0, i))] * 3,
        out_shape=[out, out, out],
        compiler_params=_cparams(("parallel",)),
        name="dn_gates",
    )(ab_t, pad(a_log), pad(dt_bias))


def _col_bcast(row):
    n = row.shape[1]
    return jnp.broadcast_to(row, (n, n)).T


def _unit_lower_inverses(nmats, ri, ci):
    def quarter(b):
        same_block = _block_id(ri, 2 * b) == _block_id(ci, 2 * b)
        return same_block & ((ri & (2 * b - 1)) >= b) & ((ci & (2 * b - 1)) < b)

    eye = jnp.where(ri == ci, 1.0, 0.0).astype(F32)
    first = quarter(1)
    xs = [eye - jnp.where(first, n, 0.0) for n in nmats]
    b = 2
    while b < CHUNK:
        mask = quarter(b)
        ys = [_dot(jnp.where(mask, n, 0.0).astype(BF16), x.astype(BF16)) for n, x in zip(nmats, xs)]
        xs = [x - _dot(x.astype(BF16), y.astype(BF16)) for x, y in zip(xs, ys)]
        b *= 2
    return xs


def _delta_body(q_ref, k_ref, v_ref, z_ref, qh_ref, kh_ref, vh_ref, wq_ref, wk_ref, wv_ref,
                gc_ref, beta_ref, gdk_ref, on_ref, o_ref, state_ref, buf_ref, *, tc, hb):
    t = pl.program_id(1)
    halo = V7X_BF16_SUBLANES
    n_super = tc // SUPER
    n_chunk = SUPER // CHUNK

    @pl.when(t == 0)
    def _():
        state_ref[...] = jnp.zeros_like(state_ref)

    states = [state_ref[h] for h in range(hb)]

    ri = lax.broadcasted_iota(jnp.int32, (SUPER, SUPER), 0)
    ci = lax.broadcasted_iota(jnp.int32, (SUPER, SUPER), 1)
    same = _block_id(ri, CHUNK) == _block_id(ci, CHUNK)
    causal = same & (ri >= ci)
    strict = same & (ri > ci)

    def conv_silu(slot, cur_ref, halo_ref, w_ref, cs):
        buf_ref[slot, 0:halo, :] = jnp.where(t == 0, 0.0, halo_ref[:, cs].astype(F32))
        buf_ref[slot, halo:halo + tc, :] = cur_ref[:, cs].astype(F32)
        y = jnp.zeros((tc, HEAD_DIM), F32)
        for j in range(CONV_K):
            off = halo - (CONV_K - 1) + j
            y = y + w_ref[j:j + 1, cs] * buf_ref[slot, off:off + tc, :]
        return _silu(y)

    def l2n(x):
        return x * lax.rsqrt(jnp.sum(x * x, axis=-1, keepdims=True) + NORM_EPS)

    heads = range(hb)
    lanes = [slice(h * HEAD_DIM, (h + 1) * HEAD_DIM) for h in heads]
    q_all = [l2n(conv_silu(3 * h, q_ref, qh_ref, wq_ref, lanes[h])) * (HEAD_DIM ** -0.5) for h in heads]
    k_all = [l2n(conv_silu(3 * h + 1, k_ref, kh_ref, wk_ref, lanes[h])) for h in heads]
    v_all = [conv_silu(3 * h + 2, v_ref, vh_ref, wv_ref, lanes[h]) for h in heads]

    units = [(h, s) for s in range(n_super) for h in heads]
    rows = lambda s: slice(s * SUPER, (s + 1) * SUPER)
    q = [q_all[h][rows(s)] for h, s in units]
    k = [k_all[h][rows(s)] for h, s in units]
    v = [v_all[h][rows(s)] for h, s in units]
    g_row = [gc_ref[h, :, rows(s)] for h, s in units]
    gdk_row = [gdk_ref[h, :, rows(s)] for h, s in units]
    g_col = [_col_bcast(g) for g in g_row]
    gdk_col = [_col_bcast(g) for g in gdk_row]
    beta_col = [_col_bcast(beta_ref[h, :, rows(s)]) for h, s in units]
    decay = [jnp.exp(jnp.where(causal, gc - jnp.broadcast_to(gr, (SUPER, SUPER)), -jnp.inf))
             for gc, gr in zip(g_col, g_row)]
    exp_g = [jnp.exp(g) for g in g_col]
    g_last = [jnp.broadcast_to(jnp.exp(gr + gd), (SUPER, SUPER)) for gr, gd in zip(g_row, gdk_row)]

    kb = [x.astype(BF16) for x in k]
    kk = [_dot_nt(x, x) for x in kb]
    qk = [_dot_nt(a.astype(BF16), b) for a, b in zip(q, kb)]
    nmat = [jnp.where(strict, b * x * d, 0.0) for b, x, d in zip(beta_col, kk, decay)]
    tinv = _unit_lower_inverses(nmat, ri, ci)
    rhs = [jnp.concatenate([kx * b * e, vx * b], axis=1).astype(BF16)
           for kx, vx, b, e in zip(k, v, beta_col, exp_g)]
    wu = [_dot(x.astype(BF16), r).astype(BF16) for x, r in zip(tinv, rhs)]
    attn = [(a * d).astype(BF16) for a, d in zip(qk, decay)]
    a_wu = [_dot(a, x) for a, x in zip(attn, wu)]
    q_eff = [(qx * e - aw[:, :HEAD_DIM]).astype(BF16) for qx, e, aw in zip(q, exp_g, a_wu)]
    o_intra = [aw[:, HEAD_DIM:] for aw in a_wu]
    kd_t = [(kx * jnp.exp(g)).T for kx, g in zip(k, gdk_col)]
    in_chunk = [_block_id(ci, CHUNK) == c for c in range(n_chunk)]
    pr = [[_dot(jnp.where(in_chunk[c], kt, 0.0).astype(BF16), x) for c in range(n_chunk)]
          for kt, x in zip(kd_t, wu)]

    outs = [[] for _ in heads]
    for s in range(n_super):
        for c in range(n_chunk):
            cr = slice(c * CHUNK, (c + 1) * CHUNK)
            for h in heads:
                u = s * hb + h
                lhs = jnp.concatenate([q_eff[u][cr], pr[u][c][:, :HEAD_DIM].astype(BF16)], axis=0)
                r = _dot(lhs, states[h].astype(BF16))
                outs[h].append(r[:CHUNK] + o_intra[u][cr])
                gl = jnp.broadcast_to(g_last[u][:, c * CHUNK:c * CHUNK + 1], (HEAD_DIM, HEAD_DIM))
                states[h] = states[h] * gl + (pr[u][c][:, HEAD_DIM:] - r[CHUNK:])

    o_norm = on_ref[...]
    for h in heads:
        o = jnp.concatenate(outs[h], axis=0)
        o = o * _rms_scale(o) * o_norm * _silu(z_ref[:, lanes[h]].astype(F32))
        o_ref[:, lanes[h]] = o.astype(o_ref.dtype)
    for h in heads:
        state_ref[h] = states[h]


def delta_core(proj, conv_w, gc, beta, gdk, o_norm, *, tc, hb):
    s = proj.shape[0]
    nh = DN_HEADS
    assert nh % hb == 0
    ng = nh // hb
    halo = V7X_BF16_SUBLANES
    nhalo = tc // halo
    width = hb * HEAD_DIM

    def cur(sec):
        return pl.BlockSpec((tc, width), lambda g, t: (t, sec * ng + g))

    def prev(sec):
        return pl.BlockSpec((halo, width), lambda g, t: (jnp.maximum(t * nhalo - 1, 0), sec * ng + g))

    def cw(sec):
        return pl.BlockSpec((CONV_K, width), lambda g, t: (0, sec * ng + g))

    gate = pl.BlockSpec((hb, 1, tc), lambda g, t: (g, 0, t))
    g3 = lambda a: a.reshape(GATE_ROWS, 1, s)
    return pl.pallas_call(
        functools.partial(_delta_body, tc=tc, hb=hb),
        grid=(ng, s // tc),
        in_specs=[cur(0), cur(1), cur(2), cur(3), prev(0), prev(1), prev(2),
                  cw(0), cw(1), cw(2), gate, gate, gate,
                  pl.BlockSpec((1, HEAD_DIM), lambda g, t: (0, 0))],
        out_specs=pl.BlockSpec((tc, width), lambda g, t: (t, g)),
        out_shape=jax.ShapeDtypeStruct((s, nh * HEAD_DIM), BF16),
        scratch_shapes=[pltpu.VMEM((hb, HEAD_DIM, HEAD_DIM), F32),
                        pltpu.VMEM((3 * hb, tc + halo, HEAD_DIM), F32)],
        compiler_params=_cparams(("parallel", "arbitrary")),
        name="delta_core",
    )(proj, proj, proj, proj, proj, proj, proj, conv_w, conv_w, conv_w,
      g3(gc), g3(beta), g3(gdk), o_norm.reshape(1, HEAD_DIM))


def _dilated_body(*refs, dils):
    ng = len(dils)
    q_refs, kc_refs, kp_refs, vc_refs, vp_refs, slope_refs = (refs[i * ng:(i + 1) * ng] for i in range(6))
    o_ref, qf, kf, vf, of, lf = refs[6 * ng:]
    n = pl.program_id(0)
    blk = DIL_BLOCK
    tt = o_ref.shape[0]
    ri = lax.broadcasted_iota(jnp.int32, (blk, 2 * blk), 0)
    ci = lax.broadcasted_iota(jnp.int32, (blk, 2 * blk), 1)
    rel = ri + blk - ci
    window = (rel >= 0) & (rel <= blk)
    window_first = window & ((ci >= blk) | (n > 0))
    rel_f = rel.astype(F32)

    for g, dil in enumerate(dils):
        span = blk * dil
        qf[...] = q_refs[g][...].astype(F32)
        kf[0:span, :] = kp_refs[g][...].astype(F32)
        kf[span:span + tt, :] = kc_refs[g][...].astype(F32)
        vf[0:span, :] = vp_refs[g][...].astype(F32)
        vf[span:span + tt, :] = vc_refs[g][...].astype(F32)
        bias = (slope_refs[g][:, 0:1] * (-float(dil))) * rel_f

        def rows(start, size, dil=dil):
            return pl.ds(start, size, stride=dil) if dil > 1 else pl.ds(start, size)

        units = [(b, c + span * b) for b in range(tt // span) for c in range(dil)]
        qs = [qf[rows(st, blk), :].astype(BF16) for _, st in units]
        ks = [kf[rows(st, 2 * blk), :].astype(BF16) for _, st in units]
        vs = [vf[rows(st, 2 * blk), :].astype(BF16) for _, st in units]
        s = [_dot_nt(a, b_) * (HEAD_DIM ** -0.5) + bias for a, b_ in zip(qs, ks)]
        s = [jnp.where(window_first if b == 0 else window, x, -jnp.inf) for (b, _), x in zip(units, s)]
        mx = [jnp.max(x, axis=-1, keepdims=True) for x in s]
        e = [jnp.exp(x - m) for x, m in zip(s, mx)]
        den = [jnp.sum(x, axis=-1, keepdims=True) for x in e]
        o = [_dot(x.astype(BF16), v) / d for x, v, d in zip(e, vs, den)]
        for (_, st), ou, m, d in zip(units, o, mx, den):
            of[g, rows(st, blk), :] = ou
            lf[g, rows(st, blk), :] = jnp.broadcast_to(m + jnp.log(d), (blk, HEAD_DIM))

    lses = [lf[g] for g in range(ng)]
    top = functools.reduce(jnp.maximum, lses)
    wts = [jnp.exp(x - top) for x in lses]
    num = sum(w * of[g] for g, w in enumerate(wts))
    o_ref[...] = (num / sum(wts)).astype(o_ref.dtype)


def dilated_attention(qproj, kv, *, tt):
    s = qproj.shape[0]
    dils = tuple(dil for _, dil in DIL_GROUPS)
    assert all(win // dil == DIL_BLOCK for win, dil in DIL_GROUPS)
    assert all(tt % (DIL_BLOCK * dil) == 0 for dil in dils) and s % tt == 0
    hpg = DIL_HEADS_PER_GROUP
    slopes = jnp.asarray([2.0 ** (-8.0 * (h + 1) / DIL_HEADS) for h in range(DIL_HEADS)], F32)
    slopes = jnp.broadcast_to(slopes.reshape(len(dils), hpg, 1, 1), (len(dils), hpg, 1, HEAD_DIM))

    def cur(col0):
        return [pl.BlockSpec((tt, HEAD_DIM), lambda n, h, g=g: (n, col0 + g * hpg + h)) for g in range(len(dils))]

    def prev(col0):
        specs = []
        for g, dil in enumerate(dils):
            span = DIL_BLOCK * dil
            specs.append(pl.BlockSpec(
                (span, HEAD_DIM),
                lambda n, h, g=g, per=tt // span: (jnp.maximum(n * per - 1, 0), col0 + g * hpg + h)))
        return specs

    slope_specs = [pl.BlockSpec((None, None, 1, HEAD_DIM), lambda n, h, g=g: (g, h, 0, 0)) for g in range(len(dils))]
    ng = len(dils)
    max_span = DIL_BLOCK * max(dils)
    return pl.pallas_call(
        functools.partial(_dilated_body, dils=dils),
        grid=(s // tt, hpg),
        in_specs=cur(0) + cur(0) + prev(0) + cur(DIL_HEADS) + prev(DIL_HEADS) + slope_specs,
        out_specs=pl.BlockSpec((tt, HEAD_DIM), lambda n, h: (n, h)),
        out_shape=jax.ShapeDtypeStruct((s, hpg * HEAD_DIM), BF16),
        scratch_shapes=[pltpu.VMEM((tt, HEAD_DIM), F32),
                        pltpu.VMEM((max_span + tt, HEAD_DIM), F32),
                        pltpu.VMEM((max_span + tt, HEAD_DIM), F32),
                        pltpu.VMEM((ng, tt, HEAD_DIM), F32),
                        pltpu.VMEM((ng, tt, HEAD_DIM), F32)],
        compiler_params=_cparams(("parallel", "arbitrary")),
        name="dilated_attention",
    )(*([qproj] * ng + [kv] * (4 * ng) + [slopes] * ng))


def _deltanet_layer(x, g_pre, g_post, mem_kv, dn_w_in, l, conv_w, a_log, dt_bias, o_norm, dn_w_out):
    qkvz = 4 * DN_WIDTH
    w_in_t = jnp.swapaxes(dn_w_in, 1, 2)
    tail = lax.optimization_barrier(w_in_t[l, qkvz:])
    w_mq_t = tail[2 * DN_HEADS:].astype(BF16)
    gate_w = lambda lo: jnp.pad(tail[lo:lo + DN_HEADS], ((0, GATE_ROWS - DN_HEADS), (0, 0)))
    w_ab_t = jnp.concatenate([gate_w(0), gate_w(DN_HEADS)], axis=0).astype(BF16)
    proj, mq, ab_t = dn_in_proj(x, g_pre, w_in_t, (l,), w_mq_t, w_ab_t, tm=1024, tn=1024)
    gc, beta, gdk = dn_gates(ab_t, a_log, dt_bias, tl=1024)
    o = delta_core(proj, conv_w.astype(F32), gc, beta, gdk, o_norm, tc=256, hb=12)
    o_mem = mem_attention(mq, 0, mem_kv, tm=512)
    return mixer_out_proj(o, o_mem, dn_w_out[l].astype(BF16), x, g_post, tm=512)


def _dilated_layer(x, g_pre, g_post, kv, mem_kv, dil_w_in, i, dil_w_out):
    proj = prenorm_matmul(x, g_pre, dil_w_in, (i,), tm=1024, tn=1024)
    o = dilated_attention(proj, kv, tt=2048)
    o_mem = mem_attention(proj, DIL_WIDTH // MEM_WIDTH, mem_kv, tm=512)
    return mixer_out_proj(o, o_mem, dil_w_out[i].astype(BF16), x, g_post, tm=512)


def kernel(x, mem, norm_gains, ffn_w_in, ffn_w_out, mem_norm_gain, w_mem_kv, dn_w_in, dn_conv, dn_a_log,
           dn_dt_bias, dn_o_norm, dn_w_out, kv_norm_gain, w_kv, dil_w_in, dil_w_out):
    batch, seq, d = x.shape
    assert batch == 1
    depth = norm_gains.shape[0]
    n_a = dn_w_in.shape[0]
    xs = x.reshape(seq, d)
    mem2 = mem.reshape(mem.shape[1], d)
    kv = None
    for l in range(depth):
        gains = norm_gains[l]
        if l == n_a:
            kv = prenorm_matmul(xs, kv_norm_gain, w_kv, tm=1024, tn=1024)
        xs = ffn(xs, gains[0], gains[1], ffn_w_in, ffn_w_out, (l, 0), tm=1024)
        mem_kv = prenorm_matmul(mem2, mem_norm_gain[l], w_mem_kv, (l,), tm=mem2.shape[0], tn=512)
        if l < n_a:
            xs = _deltanet_layer(xs, gains[2], gains[3], mem_kv, dn_w_in, l, dn_conv[l], dn_a_log[l],
                                 dn_dt_bias[l], dn_o_norm[l], dn_w_out)
        else:
            i = l - n_a
            xs = _dilated_layer(xs, gains[2], gains[3], kv, mem_kv, dil_w_in, i, dil_w_out)
        xs = ffn(xs, gains[4], gains[5], ffn_w_in, ffn_w_out, (l, 1), tm=1024)
    return xs.reshape(batch, seq, d)
```

```python
import functools
import math

import jax
import jax.numpy as jnp
from jax import lax
from jax.experimental import pallas as pl
from jax.experimental.pallas import tpu as pltpu

F32 = jnp.float32
BF16 = jnp.bfloat16

D_MODEL = 2048
HEAD_DIM = 128
DN_HEADS = 12
DN_WIDTH = DN_HEADS * HEAD_DIM
CONV_K = 4
CHUNK = 64
MEM_HEADS = 4
MEM_WIDTH = MEM_HEADS * HEAD_DIM
DIL_GROUPS = ((128, 1), (512, 4), (2048, 16))
DIL_HEADS_PER_GROUP = 4
DIL_HEADS = DIL_HEADS_PER_GROUP * len(DIL_GROUPS)
DIL_WIDTH = DIL_HEADS * HEAD_DIM
DIL_OUT = DIL_HEADS_PER_GROUP * HEAD_DIM
DIL_BLOCK = 128
D_FF = ((8 * D_MODEL // 3 + 127) // 128) * 128
NORM_EPS = 1e-6

V7X_LANES = 128
V7X_BF16_SUBLANES = 16
V7X_VMEM_BYTES = 64 * 1024 * 1024
V7X_VMEM_LIMIT_BYTES = V7X_VMEM_BYTES - 8 * 1024 * 1024
V7X_FFN_VMEM_LIMIT_BYTES = V7X_VMEM_BYTES - 2 * 1024 * 1024

FF_TILE = 512
D_FF_PAD = -(-D_FF // FF_TILE) * FF_TILE
GATE_ROWS = 16
SUPER = 2 * CHUNK


def _cparams(sem):
    return pltpu.CompilerParams(dimension_semantics=sem, vmem_limit_bytes=V7X_VMEM_LIMIT_BYTES)


def _rms_scale(x):
    return lax.rsqrt(jnp.mean(x * x, axis=-1, keepdims=True) + NORM_EPS)


def _silu(x):
    return x * jax.nn.sigmoid(x)


def _block_id(idx, size):
    assert size & (size - 1) == 0
    return lax.shift_right_logical(idx, size.bit_length() - 1)


def _dot(a, b):
    return jnp.dot(a, b, preferred_element_type=F32)


def _dot_nt(a, b):
    return lax.dot_general(a, b, (((1,), (1,)), ((), ())), preferred_element_type=F32)


def _prenorm_matmul_body(x_ref, g_ref, w_ref, o_ref, hn_ref):
    @pl.when(pl.program_id(1) == 0)
    def _():
        x = x_ref[...]
        hn_ref[...] = (x * _rms_scale(x) * g_ref[...]).astype(BF16)

    o_ref[...] = _dot(hn_ref[...], w_ref[...].astype(BF16)).astype(o_ref.dtype)


def _wspec(lead, block, index_fn):
    return pl.BlockSpec((None,) * len(lead) + block, lambda *g: lead + index_fn(*g))


def prenorm_matmul(x, gain, w, lead=(), *, tm, tn, out_dtype=BF16):
    m, d = x.shape
    n = w.shape[-1]
    return pl.pallas_call(
        _prenorm_matmul_body,
        grid=(m // tm, n // tn),
        in_specs=[
            pl.BlockSpec((tm, d), lambda i, j: (i, 0)),
            pl.BlockSpec((1, d), lambda i, j: (0, 0)),
            _wspec(lead, (d, tn), lambda i, j: (0, j)),
        ],
        out_specs=pl.BlockSpec((tm, tn), lambda i, j: (i, j)),
        out_shape=jax.ShapeDtypeStruct((m, n), out_dtype),
        scratch_shapes=[pltpu.VMEM((tm, d), BF16)],
        compiler_params=_cparams(("parallel", "arbitrary")),
        name="prenorm_matmul",
    )(x, gain.reshape(1, d), w)


def memory_kv_all_layers(mem, gains, w, *, tn):
    m, d = mem.shape
    nl, _, n = w.shape
    return pl.pallas_call(
        _prenorm_matmul_body,
        grid=(nl, n // tn),
        in_specs=[
            pl.BlockSpec((m, d), lambda l, j: (0, 0)),
            pl.BlockSpec((None, 1, d), lambda l, j: (l, 0, 0)),
            pl.BlockSpec((None, d, tn), lambda l, j: (l, 0, j)),
        ],
        out_specs=pl.BlockSpec((None, m, tn), lambda l, j: (l, 0, j)),
        out_shape=jax.ShapeDtypeStruct((nl, m, n), BF16),
        scratch_shapes=[pltpu.VMEM((m, d), BF16)],
        compiler_params=_cparams(("parallel", "arbitrary")),
        name="memory_kv",
    )(mem, gains.reshape(nl, 1, d), w)


def _dn_in_proj_body(x_ref, g_ref, w_ref, wmq_ref, wab_ref, o_ref, mq_ref, ab_ref, hn_ref):
    @pl.when(pl.program_id(1) == 0)
    def _():
        x = x_ref[...]
        hn = (x * _rms_scale(x) * g_ref[...]).astype(BF16)
        hn_ref[...] = hn
        ab_ref[...] = _dot_nt(wab_ref[...], hn)
        mq_ref[...] = _dot_nt(hn, wmq_ref[...]).astype(mq_ref.dtype)

    o_ref[...] = _dot_nt(hn_ref[...], w_ref[...].astype(BF16)).astype(o_ref.dtype)


def dn_in_proj(x, gain, w_in_t, lead, w_mq_t, w_ab_t, *, tm, tn):
    m, d = x.shape
    n_main = (4 * DN_WIDTH) // tn
    assert n_main * tn == 4 * DN_WIDTH
    nq = w_mq_t.shape[0]
    r = w_ab_t.shape[0]
    return pl.pallas_call(
        _dn_in_proj_body,
        grid=(m // tm, n_main),
        in_specs=[
            pl.BlockSpec((tm, d), lambda i, j: (i, 0)),
            pl.BlockSpec((1, d), lambda i, j: (0, 0)),
            _wspec(lead, (tn, d), lambda i, j: (j, 0)),
            pl.BlockSpec((nq, d), lambda i, j: (0, 0)),
            pl.BlockSpec((r, d), lambda i, j: (0, 0)),
        ],
        out_specs=[
            pl.BlockSpec((tm, tn), lambda i, j: (i, j)),
            pl.BlockSpec((tm, nq), lambda i, j: (i, 0)),
            pl.BlockSpec((r, tm), lambda i, j: (0, i)),
        ],
        out_shape=[
            jax.ShapeDtypeStruct((m, n_main * tn), BF16),
            jax.ShapeDtypeStruct((m, nq), BF16),
            jax.ShapeDtypeStruct((r, m), F32),
        ],
        scratch_shapes=[pltpu.VMEM((tm, d), BF16)],
        compiler_params=_cparams(("parallel", "arbitrary")),
        name="dn_in_proj",
    )(x, gain.reshape(1, d), w_in_t, w_mq_t, w_ab_t)


def _postnorm_residual(x, y, gain, scale):
    return x + scale * (y * _rms_scale(y) * gain)


FFN_ROW_GROUP = 256


def _ffn_body(x_ref, gpre_ref, g_ref, u0_ref, u12_ref, u3_ref, wo01_ref, wo2_ref, wo3_ref, gpost_ref,
              o_ref, hn_ref):
    j = pl.program_id(1)
    last = pl.num_programs(1) - 1
    tm = x_ref.shape[0]
    groups = [slice(r, r + FFN_ROW_GROUP) for r in range(0, tm, FFN_ROW_GROUP)]

    def weights(zero_tail):
        w_gu = jnp.concatenate([g_ref[...], u0_ref[...], u12_ref[...], u3_ref[...]], axis=1).astype(BF16)
        wo3 = jnp.zeros(wo3_ref.shape, wo3_ref.dtype) if zero_tail else wo3_ref[...]
        return w_gu, jnp.concatenate([wo01_ref[...], wo2_ref[...], wo3], axis=0).astype(BF16)

    def down(hn, w_gu, w_o):
        gu = _dot(hn, w_gu)
        half = gu.shape[1] // 2
        return _dot((_silu(gu[:, :half]) * gu[:, half:]).astype(BF16), w_o)

    @pl.when(j == 0)
    def _():
        w = weights(False)
        for rows in groups:
            x = x_ref[rows, :]
            hn = (x * _rms_scale(x) * gpre_ref[...]).astype(BF16)
            hn_ref[rows, :] = hn
            o_ref[rows, :] = down(hn, *w)

    @pl.when((j > 0) & (j < last))
    def _():
        o_ref[...] += down(hn_ref[...], *weights(False))

    @pl.when(j == last)
    def _():
        w = weights(True)
        for rows in groups:
            y = o_ref[rows, :] + down(hn_ref[rows, :], *w)
            o_ref[rows, :] = _postnorm_residual(x_ref[rows, :], y, gpost_ref[...], 0.5)


def ffn(x, g_pre, g_post, w_in, w_out, lead, *, tm):
    m, d = x.shape
    ff = w_out.shape[-2]
    blk = V7X_LANES
    nblk = ff // blk
    assert nblk * blk == ff and nblk % 4 == 3 and w_in.shape[-2:] == (d, 2 * ff)
    nf = (nblk + 1) // 4
    assert nf >= 2 and tm % FFN_ROW_GROUP == 0
    col = lambda width, f: _wspec(lead, (d, width * blk), lambda i, j: (0, f(j)))
    row = lambda width, f: _wspec(lead, (width * blk, d), lambda i, j: (f(j), 0))
    vec = pl.BlockSpec((1, d), lambda i, j: (0, 0))
    return pl.pallas_call(
        _ffn_body,
        grid=(m // tm, nf),
        in_specs=[
            pl.BlockSpec((tm, d), lambda i, j: (i, 0), pipeline_mode=pl.Buffered(1)),
            vec,
            col(4, lambda j: j),
            col(1, lambda j: nblk + 4 * j),
            col(2, lambda j: (nblk + 1) // 2 + 2 * j),
            col(1, lambda j: jnp.minimum(nblk + 4 * j + 3, 2 * nblk - 1)),
            row(2, lambda j: 2 * j),
            row(1, lambda j: 4 * j + 2),
            row(1, lambda j: jnp.minimum(4 * j + 3, nblk - 1)),
            vec,
        ],
        out_specs=pl.BlockSpec((tm, d), lambda i, j: (i, 0)),
        out_shape=jax.ShapeDtypeStruct((m, d), F32),
        scratch_shapes=[pltpu.VMEM((tm, d), BF16)],
        compiler_params=pltpu.CompilerParams(dimension_semantics=("parallel", "arbitrary"),
                                             vmem_limit_bytes=V7X_FFN_VMEM_LIMIT_BYTES),
        name="ffn",
    )(x, g_pre.reshape(1, d), w_in, w_in, w_in, w_in, w_out, w_out, w_out, g_post.reshape(1, d))


def _mixer_out_body(o_ref, om_ref, w1_ref, w2_ref, x_ref, g_ref, out_ref):
    y = _dot(o_ref[...], w1_ref[...]) + _dot(om_ref[...], w2_ref[...])
    out_ref[...] = _postnorm_residual(x_ref[...], y, g_ref[...], 1.0)


def mixer_out_proj(o, o_mem, w_out, x, gain, *, tm):
    m, d = x.shape
    k1, k2 = o.shape[1], o_mem.shape[1]
    assert k1 % k2 == 0 and w_out.shape == (k1 + k2, d)
    return pl.pallas_call(
        _mixer_out_body,
        grid=(m // tm,),
        in_specs=[
            pl.BlockSpec((tm, k1), lambda i: (i, 0)),
            pl.BlockSpec((tm, k2), lambda i: (i, 0)),
            pl.BlockSpec((k1, d), lambda i: (0, 0)),
            pl.BlockSpec((k2, d), lambda i: (k1 // k2, 0)),
            pl.BlockSpec((tm, d), lambda i: (i, 0)),
            pl.BlockSpec((1, d), lambda i: (0, 0)),
        ],
        out_specs=pl.BlockSpec((tm, d), lambda i: (i, 0)),
        out_shape=jax.ShapeDtypeStruct((m, d), F32),
        compiler_params=_cparams(("parallel",)),
        name="mixer_out_proj",
    )(o, o_mem, w_out, w_out, x, gain.reshape(1, d))


def _mem_attn_body(q_ref, k_ref, v_ref, o_ref):
    for h in range(MEM_HEADS):
        cs = slice(h * HEAD_DIM, (h + 1) * HEAD_DIM)
        s = _dot_nt(q_ref[:, cs], k_ref[:, cs]) * (HEAD_DIM ** -0.5)
        e = jnp.exp(s - jnp.max(s, axis=-1, keepdims=True))
        den = jnp.sum(e, axis=-1, keepdims=True)
        o_ref[:, cs] = (_dot(e.astype(BF16), v_ref[:, cs]) / den).astype(o_ref.dtype)


def mem_attention(proj, q_col_block, mem_kv, *, tm):
    m = proj.shape[0]
    mlen = mem_kv.shape[0]
    return pl.pallas_call(
        _mem_attn_body,
        grid=(m // tm,),
        in_specs=[
            pl.BlockSpec((tm, MEM_WIDTH), lambda i: (i, q_col_block)),
            pl.BlockSpec((mlen, MEM_WIDTH), lambda i: (0, 0)),
            pl.BlockSpec((mlen, MEM_WIDTH), lambda i: (0, 1)),
        ],
        out_specs=pl.BlockSpec((tm, MEM_WIDTH), lambda i: (i, 0)),
        out_shape=jax.ShapeDtypeStruct((m, MEM_WIDTH), BF16),
        compiler_params=_cparams(("parallel",)),
        name="mem_attention",
    )(proj, mem_kv, mem_kv)


def _gate_body(ab_ref, alog_ref, dtb_ref, gc_ref, beta_ref, gdk_ref):
    tl = ab_ref.shape[1]
    a = ab_ref[0:GATE_ROWS, :]
    b = ab_ref[GATE_ROWS:2 * GATE_ROWS, :]
    z = a + dtb_ref[...]
    softplus = jnp.maximum(z, 0.0) + jnp.log1p(jnp.exp(-jnp.abs(z)))
    g = -jnp.exp(alog_ref[...]) * softplus
    beta_ref[...] = jax.nn.sigmoid(b)
    r = lax.broadcasted_iota(jnp.int32, (V7X_LANES, V7X_LANES), 0)
    c = lax.broadcasted_iota(jnp.int32, (V7X_LANES, V7X_LANES), 1)
    same = _block_id(r, CHUNK) == _block_id(c, CHUNK)
    upper = jnp.where(same & (r <= c), 1.0, 0.0).astype(F32)
    ones = jnp.where(same, 1.0, 0.0).astype(F32)
    for s in range(tl // V7X_LANES):
        ls = slice(s * V7X_LANES, (s + 1) * V7X_LANES)
        gs = g[:, ls]
        gc = jnp.dot(gs, upper, preferred_element_type=F32, precision=lax.Precision.HIGHEST)
        gend = jnp.dot(gs, ones, preferred_element_type=F32, precision=lax.Precision.HIGHEST)
        gc_ref[:, ls] = gc
        gdk_ref[:, ls] = gend - gc


def dn_gates(ab_t, a_log, dt_bias, *, tl):
    r, s = ab_t.shape
    pad = lambda v: jnp.pad(v.astype(F32), (0, GATE_ROWS - v.shape[0])).reshape(GATE_ROWS, 1)
    out = jax.ShapeDtypeStruct((GATE_ROWS, s), F32)
    return pl.pallas_call(
        _gate_body,
        grid=(s // tl,),
        in_specs=[
            pl.BlockSpec((r, tl), lambda i: (0, i)),
            pl.BlockSpec((GATE_ROWS, 1), lambda i: (0, 0)),
            pl.BlockSpec((GATE_ROWS, 1), lambda i: (0, 0)),
        ],
        out_specs=[pl.BlockSpec((GATE_ROWS, tl), lambda i: (0, i))] * 3,
        out_shape=[out, out, out],
        compiler_params=_cparams(("parallel",)),
        name="dn_gates",
    )(ab_t, pad(a_log), pad(dt_bias))


def _col_bcast(row):
    n = row.shape[1]
    return jnp.broadcast_to(row, (n, n)).T


def _unit_lower_inverses(nmats, ri, ci):
    def quarter(b):
        same_block = _block_id(ri, 2 * b) == _block_id(ci, 2 * b)
        return same_block & ((ri & (2 * b - 1)) >= b) & ((ci & (2 * b - 1)) < b)

    eye = jnp.where(ri == ci, 1.0, 0.0).astype(F32)
    first = quarter(1)
    xs = [eye - jnp.where(first, n, 0.0) for n in nmats]
    b = 2
    while b < CHUNK:
        mask = quarter(b)
        ys = [_dot(jnp.where(mask, n, 0.0).astype(BF16), x.astype(BF16)) for n, x in zip(nmats, xs)]
        xs = [x - _dot(x.astype(BF16), y.astype(BF16)) for x, y in zip(xs, ys)]
        b *= 2
    return xs


def _delta_body(q_ref, k_ref, v_ref, z_ref, qh_ref, kh_ref, vh_ref, wq_ref, wk_ref, wv_ref,
                gc_ref, beta_ref, gdk_ref, on_ref, o_ref, state_ref, buf_ref, *, tc, hb):
    t = pl.program_id(1)
    halo = V7X_BF16_SUBLANES
    n_super = tc // SUPER
    n_chunk = SUPER // CHUNK

    @pl.when(t == 0)
    def _():
        state_ref[...] = jnp.zeros_like(state_ref)

    states = [state_ref[h] for h in range(hb)]

    ri = lax.broadcasted_iota(jnp.int32, (SUPER, SUPER), 0)
    ci = lax.broadcasted_iota(jnp.int32, (SUPER, SUPER), 1)
    same = _block_id(ri, CHUNK) == _block_id(ci, CHUNK)
    causal = same & (ri >= ci)
    strict = same & (ri > ci)

    def conv_silu(slot, cur_ref, halo_ref, w_ref, cs):
        buf_ref[slot, 0:halo, :] = jnp.where(t == 0, 0.0, halo_ref[:, cs].astype(F32))
        buf_ref[slot, halo:halo + tc, :] = cur_ref[:, cs].astype(F32)
        y = jnp.zeros((tc, HEAD_DIM), F32)
        for j in range(CONV_K):
            off = halo - (CONV_K - 1) + j
            y = y + w_ref[j:j + 1, cs] * buf_ref[slot, off:off + tc, :]
        return _silu(y)

    def l2n(x):
        return x * lax.rsqrt(jnp.sum(x * x, axis=-1, keepdims=True) + NORM_EPS)

    heads = range(hb)
    lanes = [slice(h * HEAD_DIM, (h + 1) * HEAD_DIM) for h in heads]
    q_all = [l2n(conv_silu(3 * h, q_ref, qh_ref, wq_ref, lanes[h])) * (HEAD_DIM ** -0.5) for h in heads]
    k_all = [l2n(conv_silu(3 * h + 1, k_ref, kh_ref, wk_ref, lanes[h])) for h in heads]
    v_all = [conv_silu(3 * h + 2, v_ref, vh_ref, wv_ref, lanes[h]) for h in heads]

    units = [(h, s) for s in range(n_super) for h in heads]
    rows = lambda s: slice(s * SUPER, (s + 1) * SUPER)
    q = [q_all[h][rows(s)] for h, s in units]
    k = [k_all[h][rows(s)] for h, s in units]
    v = [v_all[h][rows(s)] for h, s in units]
    g_row = [gc_ref[h, :, rows(s)] for h, s in units]
    gdk_row = [gdk_ref[h, :, rows(s)] for h, s in units]
    g_col = [_col_bcast(g) for g in g_row]
    gdk_col = [_col_bcast(g) for g in gdk_row]
    beta_col = [_col_bcast(beta_ref[h, :, rows(s)]) for h, s in units]
    decay = [jnp.exp(jnp.where(causal, gc - jnp.broadcast_to(gr, (SUPER, SUPER)), -jnp.inf))
             for gc, gr in zip(g_col, g_row)]
    exp_g = [jnp.exp(g) for g in g_col]
    g_last = [jnp.broadcast_to(jnp.exp(gr + gd), (SUPER, SUPER)) for gr, gd in zip(g_row, gdk_row)]

    kb = [x.astype(BF16) for x in k]
    kk = [_dot_nt(x, x) for x in kb]
    qk = [_dot_nt(a.astype(BF16), b) for a, b in zip(q, kb)]
    nmat = [jnp.where(strict, b * x * d, 0.0) for b, x, d in zip(beta_col, kk, decay)]
    tinv = _unit_lower_inverses(nmat, ri, ci)
    rhs = [jnp.concatenate([kx * b * e, vx * b], axis=1).astype(BF16)
           for kx, vx, b, e in zip(k, v, beta_col, exp_g)]
    wu = [_dot(x.astype(BF16), r).astype(BF16) for x, r in zip(tinv, rhs)]
    attn = [(a * d).astype(BF16) for a, d in zip(qk, decay)]
    a_wu = [_dot(a, x) for a, x in zip(attn, wu)]
    q_eff = [(qx * e - aw[:, :HEAD_DIM]).astype(BF16) for qx, e, aw in zip(q, exp_g, a_wu)]
    o_intra = [aw[:, HEAD_DIM:] for aw in a_wu]
    kd_t = [(kx * jnp.exp(g)).T for kx, g in zip(k, gdk_col)]
    in_chunk = [_block_id(ci, CHUNK) == c for c in range(n_chunk)]
    pr = [[_dot(jnp.where(in_chunk[c], kt, 0.0).astype(BF16), x) for c in range(n_chunk)]
          for kt, x in zip(kd_t, wu)]

    outs = [[] for _ in heads]
    for s in range(n_super):
        for c in range(n_chunk):
            cr = slice(c * CHUNK, (c + 1) * CHUNK)
            for h in heads:
                u = s * hb + h
                lhs = jnp.concatenate([q_eff[u][cr], pr[u][c][:, :HEAD_DIM].astype(BF16)], axis=0)
                r = _dot(lhs, states[h].astype(BF16))
                outs[h].append(r[:CHUNK] + o_intra[u][cr])
                gl = jnp.broadcast_to(g_last[u][:, c * CHUNK:c * CHUNK + 1], (HEAD_DIM, HEAD_DIM))
                states[h] = states[h] * gl + (pr[u][c][:, HEAD_DIM:] - r[CHUNK:])

    o_norm = on_ref[...]
    for h in heads:
        o = jnp.concatenate(outs[h], axis=0)
        o = o * _rms_scale(o) * o_norm * _silu(z_ref[:, lanes[h]].astype(F32))
        o_ref[:, lanes[h]] = o.astype(o_ref.dtype)
    for h in heads:
        state_ref[h] = states[h]


def delta_core(proj, conv_w, gc, beta, gdk, o_norm, *, tc, hb):
    s = proj.shape[0]
    nh = DN_HEADS
    assert nh % hb == 0
    ng = nh // hb
    halo = V7X_BF16_SUBLANES
    nhalo = tc // halo
    width = hb * HEAD_DIM

    def cur(sec):
        return pl.BlockSpec((tc, width), lambda g, t: (t, sec * ng + g))

    def prev(sec):
        return pl.BlockSpec((halo, width), lambda g, t: (jnp.maximum(t * nhalo - 1, 0), sec * ng + g))

    def cw(sec):
        return pl.BlockSpec((CONV_K, width), lambda g, t: (0, sec * ng + g))

    gate = pl.BlockSpec((hb, 1, tc), lambda g, t: (g, 0, t))
    g3 = lambda a: a.reshape(GATE_ROWS, 1, s)
    return pl.pallas_call(
        functools.partial(_delta_body, tc=tc, hb=hb),
        grid=(ng, s // tc),
        in_specs=[cur(0), cur(1), cur(2), cur(3), prev(0), prev(1), prev(2),
                  cw(0), cw(1), cw(2), gate, gate, gate,
                  pl.BlockSpec((1, HEAD_DIM), lambda g, t: (0, 0))],
        out_specs=pl.BlockSpec((tc, width), lambda g, t: (t, g)),
        out_shape=jax.ShapeDtypeStruct((s, nh * HEAD_DIM), BF16),
        scratch_shapes=[pltpu.VMEM((hb, HEAD_DIM, HEAD_DIM), F32),
                        pltpu.VMEM((3 * hb, tc + halo, HEAD_DIM), F32)],
        compiler_params=_cparams(("parallel", "arbitrary")),
        name="delta_core",
    )(proj, proj, proj, proj, proj, proj, proj, conv_w, conv_w, conv_w,
      g3(gc), g3(beta), g3(gdk), o_norm.reshape(1, HEAD_DIM))


def _dilated_body(*refs, dils):
    ng = len(dils)
    q_refs, kc_refs, kp_refs, vc_refs, vp_refs, slope_refs = (refs[i * ng:(i + 1) * ng] for i in range(6))
    o_ref, qf, kf, vf, of, lf = refs[6 * ng:]
    n = pl.program_id(0)
    blk = DIL_BLOCK
    tt = o_ref.shape[0]
    ri = lax.broadcasted_iota(jnp.int32, (blk, 2 * blk), 0)
    ci = lax.broadcasted_iota(jnp.int32, (blk, 2 * blk), 1)
    rel = ri + blk - ci
    window = (rel >= 0) & (rel <= blk)
    window_first = window & ((ci >= blk) | (n > 0))
    rel_f = rel.astype(F32)

    for g, dil in enumerate(dils):
        span = blk * dil
        qf[...] = q_refs[g][...].astype(F32)
        kf[0:span, :] = kp_refs[g][...].astype(F32)
        kf[span:span + tt, :] = kc_refs[g][...].astype(F32)
        vf[0:span, :] = vp_refs[g][...].astype(F32)
        vf[span:span + tt, :] = vc_refs[g][...].astype(F32)
        bias = (slope_refs[g][:, 0:1] * (-float(dil))) * rel_f

        def rows(start, size, dil=dil):
            return pl.ds(start, size, stride=dil) if dil > 1 else pl.ds(start, size)

        units = [(b, c + span * b) for b in range(tt // span) for c in range(dil)]
        qs = [qf[rows(st, blk), :].astype(BF16) for _, st in units]
        ks = [kf[rows(st, 2 * blk), :].astype(BF16) for _, st in units]
        vs = [vf[rows(st, 2 * blk), :].astype(BF16) for _, st in units]
        s = [_dot_nt(a, b_) * (HEAD_DIM ** -0.5) + bias for a, b_ in zip(qs, ks)]
        s = [jnp.where(window_first if b == 0 else window, x, -jnp.inf) for (b, _), x in zip(units, s)]
        mx = [jnp.max(x, axis=-1, keepdims=True) for x in s]
        e = [jnp.exp(x - m) for x, m in zip(s, mx)]
        den = [jnp.sum(x, axis=-1, keepdims=True) for x in e]
        o = [_dot(x.astype(BF16), v) / d for x, v, d in zip(e, vs, den)]
        for (_, st), ou, m, d in zip(units, o, mx, den):
            of[g, rows(st, blk), :] = ou
            lf[g, rows(st, blk), :] = jnp.broadcast_to(m + jnp.log(d), (blk, HEAD_DIM))

    lses = [lf[g] for g in range(ng)]
    top = functools.reduce(jnp.maximum, lses)
    wts = [jnp.exp(x - top) for x in lses]
    num = sum(w * of[g] for g, w in enumerate(wts))
    o_ref[...] = (num / sum(wts)).astype(o_ref.dtype)


def dilated_attention(qproj, kv, *, tt):
    s = qproj.shape[0]
    dils = tuple(dil for _, dil in DIL_GROUPS)
    assert all(win // dil == DIL_BLOCK for win, dil in DIL_GROUPS)
    assert all(tt % (DIL_BLOCK * dil) == 0 for dil in dils) and s % tt == 0
    hpg = DIL_HEADS_PER_GROUP
    slopes = jnp.asarray([2.0 ** (-8.0 * (h + 1) / DIL_HEADS) for h in range(DIL_HEADS)], F32)
    slopes = jnp.broadcast_to(slopes.reshape(len(dils), hpg, 1, 1), (len(dils), hpg, 1, HEAD_DIM))

    def cur(col0):
        return [pl.BlockSpec((tt, HEAD_DIM), lambda n, h, g=g: (n, col0 + g * hpg + h)) for g in range(len(dils))]

    def prev(col0):
        specs = []
        for g, dil in enumerate(dils):
            span = DIL_BLOCK * dil
            specs.append(pl.BlockSpec(
                (span, HEAD_DIM),
                lambda n, h, g=g, per=tt // span: (jnp.maximum(n * per - 1, 0), col0 + g * hpg + h)))
        return specs

    slope_specs = [pl.BlockSpec((None, None, 1, HEAD_DIM), lambda n, h, g=g: (g, h, 0, 0)) for g in range(len(dils))]
    ng = len(dils)
    max_span = DIL_BLOCK * max(dils)
    return pl.pallas_call(
        functools.partial(_dilated_body, dils=dils),
        grid=(s // tt, hpg),
        in_specs=cur(0) + cur(0) + prev(0) + cur(DIL_HEADS) + prev(DIL_HEADS) + slope_specs,
        out_specs=pl.BlockSpec((tt, HEAD_DIM), lambda n, h: (n, h)),
        out_shape=jax.ShapeDtypeStruct((s, hpg * HEAD_DIM), BF16),
        scratch_shapes=[pltpu.VMEM((tt, HEAD_DIM), F32),
                        pltpu.VMEM((max_span + tt, HEAD_DIM), F32),
                        pltpu.VMEM((max_span + tt, HEAD_DIM), F32),
                        pltpu.VMEM((ng, tt, HEAD_DIM), F32),
                        pltpu.VMEM((ng, tt, HEAD_DIM), F32)],
        compiler_params=_cparams(("parallel", "arbitrary")),
        name="dilated_attention",
    )(*([qproj] * ng + [kv] * (4 * ng) + [slopes] * ng))


def _deltanet_layer(x, g_pre, g_post, mem_kv, dn_w_in, l, conv_w, a_log, dt_bias, o_norm, dn_w_out):
    qkvz = 4 * DN_WIDTH
    w_in_t = jnp.swapaxes(dn_w_in, 1, 2)
    tail = lax.optimization_barrier(w_in_t[l, qkvz:])
    w_mq_t = tail[2 * DN_HEADS:].astype(BF16)
    gate_w = lambda lo: jnp.pad(tail[lo:lo + DN_HEADS], ((0, GATE_ROWS - DN_HEADS), (0, 0)))
    w_ab_t = jnp.concatenate([gate_w(0), gate_w(DN_HEADS)], axis=0).astype(BF16)
    proj, mq, ab_t = dn_in_proj(x, g_pre, w_in_t, (l,), w_mq_t, w_ab_t, tm=1024, tn=1024)
    gc, beta, gdk = dn_gates(ab_t, a_log, dt_bias, tl=1024)
    o = delta_core(proj, conv_w.astype(F32), gc, beta, gdk, o_norm, tc=256, hb=12)
    o_mem = mem_attention(mq, 0, mem_kv, tm=512)
    return mixer_out_proj(o, o_mem, dn_w_out[l].astype(BF16), x, g_post, tm=512)


def _dilated_layer(x, g_pre, g_post, kv, mem_kv, dil_w_in, i, dil_w_out):
    proj = prenorm_matmul(x, g_pre, dil_w_in, (i,), tm=1024, tn=1024)
    o = dilated_attention(proj, kv, tt=2048)
    o_mem = mem_attention(proj, DIL_WIDTH // MEM_WIDTH, mem_kv, tm=512)
    return mixer_out_proj(o, o_mem, dil_w_out[i].astype(BF16), x, g_post, tm=512)


def kernel(x, mem, norm_gains, ffn_w_in, ffn_w_out, mem_norm_gain, w_mem_kv, dn_w_in, dn_conv, dn_a_log,
           dn_dt_bias, dn_o_norm, dn_w_out, kv_norm_gain, w_kv, dil_w_in, dil_w_out):
    batch, seq, d = x.shape
    assert batch == 1
    depth = norm_gains.shape[0]
    n_a = dn_w_in.shape[0]
    xs = x.reshape(seq, d)
    mem2 = mem.reshape(mem.shape[1], d)
    mem_kv_all = memory_kv_all_layers(mem2, mem_norm_gain, w_mem_kv, tn=512)
    kv = None
    for l in range(depth):
        gains = norm_gains[l]
        if l == n_a:
            kv = prenorm_matmul(xs, kv_norm_gain, w_kv, tm=1024, tn=1024)
        xs = ffn(xs, gains[0], gains[1], ffn_w_in, ffn_w_out, (l, 0), tm=1024)
        mem_kv = mem_kv_all[l]
        if l < n_a:
            xs = _deltanet_layer(xs, gains[2], gains[3], mem_kv, dn_w_in, l, dn_conv[l], dn_a_log[l],
                                 dn_dt_bias[l], dn_o_norm[l], dn_w_out)
        else:
            i = l - n_a
            xs = _dilated_layer(xs, gains[2], gains[3], kv, mem_kv, dil_w_in, i, dil_w_out)
        xs = ffn(xs, gains[4], gains[5], ffn_w_in, ffn_w_out, (l, 1), tm=1024)
    return xs.reshape(batch, seq, d)
```

```python
import functools
import math

import jax
import jax.numpy as jnp
from jax import lax
from jax.experimental import pallas as pl
from jax.experimental.pallas import tpu as pltpu

F32 = jnp.float32
BF16 = jnp.bfloat16

D_MODEL = 2048
HEAD_DIM = 128
DN_HEADS = 12
DN_WIDTH = DN_HEADS * HEAD_DIM
CONV_K = 4
CHUNK = 64
MEM_HEADS = 4
MEM_WIDTH = MEM_HEADS * HEAD_DIM
DIL_GROUPS = ((128, 1), (512, 4), (2048, 16))
DIL_HEADS_PER_GROUP = 4
DIL_HEADS = DIL_HEADS_PER_GROUP * len(DIL_GROUPS)
DIL_WIDTH = DIL_HEADS * HEAD_DIM
DIL_OUT = DIL_HEADS_PER_GROUP * HEAD_DIM
DIL_BLOCK = 128
D_FF = ((8 * D_MODEL // 3 + 127) // 128) * 128
NORM_EPS = 1e-6

V7X_LANES = 128
V7X_BF16_SUBLANES = 16
V7X_VMEM_LIMIT_BYTES = 56 * 1024 * 1024

FF_TILE = 512
D_FF_PAD = -(-D_FF // FF_TILE) * FF_TILE
GATE_ROWS = 16
SUPER = 2 * CHUNK


def _cparams(sem):
    return pltpu.CompilerParams(dimension_semantics=sem, vmem_limit_bytes=V7X_VMEM_LIMIT_BYTES)


def _rms_scale(x):
    return lax.rsqrt(jnp.mean(x * x, axis=-1, keepdims=True) + NORM_EPS)


def _silu(x):
    return x * jax.nn.sigmoid(x)


def _block_id(idx, size):
    assert size & (size - 1) == 0
    return lax.shift_right_logical(idx, size.bit_length() - 1)


def _dot(a, b):
    return jnp.dot(a, b, preferred_element_type=F32)


def _dot_nt(a, b):
    return lax.dot_general(a, b, (((1,), (1,)), ((), ())), preferred_element_type=F32)


NORM_ROW_GROUP = 256


def _row_groups(tm):
    step = min(tm, NORM_ROW_GROUP)
    return [slice(r, r + step) for r in range(0, tm, step)]


def _prenorm_matmul_body(x_ref, g_ref, w_ref, o_ref, hn_ref):
    j = pl.program_id(1)

    @pl.when(j == 0)
    def _():
        w = w_ref[...].astype(BF16)
        for rows in _row_groups(x_ref.shape[0]):
            x = x_ref[rows, :]
            hn = (x * _rms_scale(x) * g_ref[...]).astype(BF16)
            hn_ref[rows, :] = hn
            o_ref[rows, :] = _dot(hn, w).astype(o_ref.dtype)

    @pl.when(j > 0)
    def _():
        o_ref[...] = _dot(hn_ref[...], w_ref[...].astype(BF16)).astype(o_ref.dtype)


def _wspec(lead, block, index_fn):
    return pl.BlockSpec((None,) * len(lead) + block, lambda *g: lead + index_fn(*g))


def prenorm_matmul(x, gain, w, lead=(), *, tm, tn, out_dtype=BF16):
    m, d = x.shape
    n = w.shape[-1]
    return pl.pallas_call(
        _prenorm_matmul_body,
        grid=(m // tm, n // tn),
        in_specs=[
            pl.BlockSpec((tm, d), lambda i, j: (i, 0)),
            pl.BlockSpec((1, d), lambda i, j: (0, 0)),
            _wspec(lead, (d, tn), lambda i, j: (0, j)),
        ],
        out_specs=pl.BlockSpec((tm, tn), lambda i, j: (i, j)),
        out_shape=jax.ShapeDtypeStruct((m, n), out_dtype),
        scratch_shapes=[pltpu.VMEM((tm, d), BF16)],
        compiler_params=_cparams(("parallel", "arbitrary")),
        name="prenorm_matmul",
    )(x, gain.reshape(1, d), w)


def memory_kv_all_layers(mem, gains, w, *, tn):
    m, d = mem.shape
    nl, _, n = w.shape
    return pl.pallas_call(
        _prenorm_matmul_body,
        grid=(nl, n // tn),
        in_specs=[
            pl.BlockSpec((m, d), lambda l, j: (0, 0)),
            pl.BlockSpec((None, 1, d), lambda l, j: (l, 0, 0)),
            pl.BlockSpec((None, d, tn), lambda l, j: (l, 0, j)),
        ],
        out_specs=pl.BlockSpec((None, m, tn), lambda l, j: (l, 0, j)),
        out_shape=jax.ShapeDtypeStruct((nl, m, n), BF16),
        scratch_shapes=[pltpu.VMEM((m, d), BF16)],
        compiler_params=_cparams(("parallel", "arbitrary")),
        name="memory_kv",
    )(mem, gains.reshape(nl, 1, d), w)


def _dn_in_proj_body(x_ref, g_ref, w_ref, wmq_ref, wab_ref, o_ref, mq_ref, ab_ref, hn_ref):
    j = pl.program_id(1)

    @pl.when(j == 0)
    def _():
        w = w_ref[...].astype(BF16)
        for rows in _row_groups(x_ref.shape[0]):
            x = x_ref[rows, :]
            hn = (x * _rms_scale(x) * g_ref[...]).astype(BF16)
            hn_ref[rows, :] = hn
            ab_ref[:, rows] = _dot_nt(wab_ref[...], hn)
            mq_ref[rows, :] = _dot_nt(hn, wmq_ref[...]).astype(mq_ref.dtype)
            o_ref[rows, :] = _dot_nt(hn, w).astype(o_ref.dtype)

    @pl.when(j > 0)
    def _():
        o_ref[...] = _dot_nt(hn_ref[...], w_ref[...].astype(BF16)).astype(o_ref.dtype)


def dn_in_proj(x, gain, w_in_t, lead, w_mq_t, w_ab_t, *, tm, tn):
    m, d = x.shape
    n_main = (4 * DN_WIDTH) // tn
    assert n_main * tn == 4 * DN_WIDTH
    nq = w_mq_t.shape[0]
    r = w_ab_t.shape[0]
    return pl.pallas_call(
        _dn_in_proj_body,
        grid=(m // tm, n_main),
        in_specs=[
            pl.BlockSpec((tm, d), lambda i, j: (i, 0)),
            pl.BlockSpec((1, d), lambda i, j: (0, 0)),
            _wspec(lead, (tn, d), lambda i, j: (j, 0)),
            pl.BlockSpec((nq, d), lambda i, j: (0, 0)),
            pl.BlockSpec((r, d), lambda i, j: (0, 0)),
        ],
        out_specs=[
            pl.BlockSpec((tm, tn), lambda i, j: (i, j)),
            pl.BlockSpec((tm, nq), lambda i, j: (i, 0)),
            pl.BlockSpec((r, tm), lambda i, j: (0, i)),
        ],
        out_shape=[
            jax.ShapeDtypeStruct((m, n_main * tn), BF16),
            jax.ShapeDtypeStruct((m, nq), BF16),
            jax.ShapeDtypeStruct((r, m), F32),
        ],
        scratch_shapes=[pltpu.VMEM((tm, d), BF16)],
        compiler_params=_cparams(("parallel", "arbitrary")),
        name="dn_in_proj",
    )(x, gain.reshape(1, d), w_in_t, w_mq_t, w_ab_t)


def _postnorm_residual(x, y, gain, scale):
    return x + scale * (y * _rms_scale(y) * gain)


FFN_ROW_GROUP = 256


def _ffn_body(x_ref, gpre_ref, ga_ref, gb_ref, ua_ref, ub_ref, wa_ref, wb_ref, gpost_ref, o_ref, hn_ref,
              *, odd_tail):
    j = pl.program_id(1)
    last = pl.num_programs(1) - 1
    tm = x_ref.shape[0]
    groups = [slice(r, r + FFN_ROW_GROUP) for r in range(0, tm, FFN_ROW_GROUP)]

    def weights(zero_tail):
        w_gu = jnp.concatenate([ga_ref[...], gb_ref[...], ua_ref[...], ub_ref[...]], axis=1).astype(BF16)
        wb = jnp.zeros(wb_ref.shape, wb_ref.dtype) if zero_tail else wb_ref[...]
        return w_gu, jnp.concatenate([wa_ref[...], wb], axis=0).astype(BF16)

    def down(hn, w_gu, w_o):
        gu = _dot(hn, w_gu)
        half = gu.shape[1] // 2
        return _dot((_silu(gu[:, :half]) * gu[:, half:]).astype(BF16), w_o)

    @pl.when(j == 0)
    def _():
        w = weights(False)
        for rows in groups:
            x = x_ref[rows, :]
            hn = (x * _rms_scale(x) * gpre_ref[...]).astype(BF16)
            hn_ref[rows, :] = hn
            o_ref[rows, :] = down(hn, *w)

    @pl.when((j > 0) & (j < last))
    def _():
        o_ref[...] += down(hn_ref[...], *weights(False))

    @pl.when(j == last)
    def _():
        w = weights(odd_tail)
        for rows in groups:
            y = o_ref[rows, :] + down(hn_ref[rows, :], *w)
            o_ref[rows, :] = _postnorm_residual(x_ref[rows, :], y, gpost_ref[...], 0.5)


def ffn(x, g_pre, g_post, w_in, w_out, lead, *, tm):
    m, d = x.shape
    ff = w_out.shape[-2]
    blk = V7X_LANES
    nblk = ff // blk
    assert nblk * blk == ff and w_in.shape[-2:] == (d, 2 * ff)
    nf = -(-nblk // 2)
    assert nf >= 2 and tm % FFN_ROW_GROUP == 0
    second = lambda j: jnp.minimum(2 * j + 1, nblk - 1)
    col = lambda f: _wspec(lead, (d, blk), lambda i, j: (0, f(j)))
    row = lambda f: _wspec(lead, (blk, d), lambda i, j: (f(j), 0))
    vec = pl.BlockSpec((1, d), lambda i, j: (0, 0))
    return pl.pallas_call(
        functools.partial(_ffn_body, odd_tail=bool(nblk % 2)),
        grid=(m // tm, nf),
        in_specs=[
            pl.BlockSpec((tm, d), lambda i, j: (i, 0)),
            vec,
            col(lambda j: 2 * j), col(second),
            col(lambda j: nblk + 2 * j), col(lambda j: nblk + second(j)),
            row(lambda j: 2 * j), row(second),
            vec,
        ],
        out_specs=pl.BlockSpec((tm, d), lambda i, j: (i, 0)),
        out_shape=jax.ShapeDtypeStruct((m, d), F32),
        scratch_shapes=[pltpu.VMEM((tm, d), BF16)],
        compiler_params=_cparams(("parallel", "arbitrary")),
        name="ffn",
    )(x, g_pre.reshape(1, d), w_in, w_in, w_in, w_in, w_out, w_out, g_post.reshape(1, d))


def _mixer_out_body(o_ref, mq_ref, mk_ref, mv_ref, w1_ref, w2_ref, x_ref, g_ref, out_ref):
    heads = []
    for h in range(MEM_HEADS):
        cs = slice(h * HEAD_DIM, (h + 1) * HEAD_DIM)
        s = _dot_nt(mq_ref[:, cs], mk_ref[:, cs]) * (HEAD_DIM ** -0.5)
        e = jnp.exp(s - jnp.max(s, axis=-1, keepdims=True))
        den = jnp.sum(e, axis=-1, keepdims=True)
        heads.append((_dot(e.astype(BF16), mv_ref[:, cs]) / den).astype(BF16))
    o_mem = jnp.concatenate(heads, axis=1)
    y = _dot(o_ref[...], w1_ref[...]) + _dot(o_mem, w2_ref[...])
    out_ref[...] = _postnorm_residual(x_ref[...], y, g_ref[...], 1.0)


def mixer_out_proj(o, mq_src, mq_col_block, mem_kv, w_out, x, gain, *, tm):
    m, d = x.shape
    k1, k2 = o.shape[1], MEM_WIDTH
    mlen = mem_kv.shape[0]
    assert k1 % k2 == 0 and w_out.shape == (k1 + k2, d)
    return pl.pallas_call(
        _mixer_out_body,
        grid=(m // tm,),
        in_specs=[
            pl.BlockSpec((tm, k1), lambda i: (i, 0)),
            pl.BlockSpec((tm, k2), lambda i: (i, mq_col_block)),
            pl.BlockSpec((mlen, k2), lambda i: (0, 0)),
            pl.BlockSpec((mlen, k2), lambda i: (0, 1)),
            pl.BlockSpec((k1, d), lambda i: (0, 0)),
            pl.BlockSpec((k2, d), lambda i: (k1 // k2, 0)),
            pl.BlockSpec((tm, d), lambda i: (i, 0)),
            pl.BlockSpec((1, d), lambda i: (0, 0)),
        ],
        out_specs=pl.BlockSpec((tm, d), lambda i: (i, 0)),
        out_shape=jax.ShapeDtypeStruct((m, d), F32),
        compiler_params=_cparams(("parallel",)),
        name="mixer_out_proj",
    )(o, mq_src, mem_kv, mem_kv, w_out, w_out, x, gain.reshape(1, d))


def _gate_body(ab_ref, alog_ref, dtb_ref, gc_ref, beta_ref, gdk_ref):
    tl = ab_ref.shape[1]
    a = ab_ref[0:GATE_ROWS, :]
    b = ab_ref[GATE_ROWS:2 * GATE_ROWS, :]
    z = a + dtb_ref[...]
    softplus = jnp.maximum(z, 0.0) + jnp.log1p(jnp.exp(-jnp.abs(z)))
    g = -jnp.exp(alog_ref[...]) * softplus
    beta_ref[...] = jax.nn.sigmoid(b)
    r = lax.broadcasted_iota(jnp.int32, (V7X_LANES, V7X_LANES), 0)
    c = lax.broadcasted_iota(jnp.int32, (V7X_LANES, V7X_LANES), 1)
    same = _block_id(r, CHUNK) == _block_id(c, CHUNK)
    upper = jnp.where(same & (r <= c), 1.0, 0.0).astype(F32)
    ones = jnp.where(same, 1.0, 0.0).astype(F32)
    for s in range(tl // V7X_LANES):
        ls = slice(s * V7X_LANES, (s + 1) * V7X_LANES)
        gs = g[:, ls]
        gc = jnp.dot(gs, upper, preferred_element_type=F32, precision=lax.Precision.HIGHEST)
        gend = jnp.dot(gs, ones, preferred_element_type=F32, precision=lax.Precision.HIGHEST)
        gc_ref[:, ls] = gc
        gdk_ref[:, ls] = gend - gc


def dn_gates(ab_t, a_log, dt_bias, *, tl):
    r, s = ab_t.shape
    pad = lambda v: jnp.pad(v.astype(F32), (0, GATE_ROWS - v.shape[0])).reshape(GATE_ROWS, 1)
    out = jax.ShapeDtypeStruct((GATE_ROWS, s), F32)
    return pl.pallas_call(
        _gate_body,
        grid=(s // tl,),
        in_specs=[
            pl.BlockSpec((r, tl), lambda i: (0, i)),
            pl.BlockSpec((GATE_ROWS, 1), lambda i: (0, 0)),
            pl.BlockSpec((GATE_ROWS, 1), lambda i: (0, 0)),
        ],
        out_specs=[pl.BlockSpec((GATE_ROWS, tl), lambda i: (0, i))] * 3,
        out_shape=[out, out, out],
        compiler_params=_cparams(("parallel",)),
        name="dn_gates",
    )(ab_t, pad(a_log), pad(dt_bias))


def _col_bcast(row):
    n = row.shape[1]
    return jnp.broadcast_to(row, (n, n)).T


def _unit_lower_inverses(nmats, ri, ci):
    def quarter(b):
        same_block = _block_id(ri, 2 * b) == _block_id(ci, 2 * b)
        return same_block & ((ri & (2 * b - 1)) >= b) & ((ci & (2 * b - 1)) < b)

    eye = jnp.where(ri == ci, 1.0, 0.0).astype(F32)
    first = quarter(1)
    xs = [eye - jnp.where(first, n, 0.0) for n in nmats]
    b = 2
    while b < CHUNK:
        mask = quarter(b)
        ys = [_dot(jnp.where(mask, n, 0.0).astype(BF16), x.astype(BF16)) for n, x in zip(nmats, xs)]
        xs = [x - _dot(x.astype(BF16), y.astype(BF16)) for x, y in zip(xs, ys)]
        b *= 2
    return xs


def _delta_body(q_ref, k_ref, v_ref, z_ref, qh_ref, kh_ref, vh_ref, wq_ref, wk_ref, wv_ref,
                gc_ref, beta_ref, gdk_ref, on_ref, o_ref, state_ref, buf_ref, *, tc, hb):
    t = pl.program_id(1)
    halo = V7X_BF16_SUBLANES
    n_super = tc // SUPER
    n_chunk = SUPER // CHUNK

    @pl.when(t == 0)
    def _():
        state_ref[...] = jnp.zeros_like(state_ref)

    states = [state_ref[h] for h in range(hb)]

    ri = lax.broadcasted_iota(jnp.int32, (SUPER, SUPER), 0)
    ci = lax.broadcasted_iota(jnp.int32, (SUPER, SUPER), 1)
    same = _block_id(ri, CHUNK) == _block_id(ci, CHUNK)
    causal = same & (ri >= ci)
    strict = same & (ri > ci)

    def conv_silu(slot, cur_ref, halo_ref, w_ref, cs):
        buf_ref[slot, 0:halo, :] = jnp.where(t == 0, 0.0, halo_ref[:, cs].astype(F32))
        buf_ref[slot, halo:halo + tc, :] = cur_ref[:, cs].astype(F32)
        y = jnp.zeros((tc, HEAD_DIM), F32)
        for j in range(CONV_K):
            off = halo - (CONV_K - 1) + j
            y = y + w_ref[j:j + 1, cs] * buf_ref[slot, off:off + tc, :]
        return _silu(y)

    def l2n(x):
        return x * lax.rsqrt(jnp.sum(x * x, axis=-1, keepdims=True) + NORM_EPS)

    heads = range(hb)
    lanes = [slice(h * HEAD_DIM, (h + 1) * HEAD_DIM) for h in heads]
    q_all = [l2n(conv_silu(3 * h, q_ref, qh_ref, wq_ref, lanes[h])) * (HEAD_DIM ** -0.5) for h in heads]
    k_all = [l2n(conv_silu(3 * h + 1, k_ref, kh_ref, wk_ref, lanes[h])) for h in heads]
    v_all = [conv_silu(3 * h + 2, v_ref, vh_ref, wv_ref, lanes[h]) for h in heads]

    units = [(h, s) for s in range(n_super) for h in heads]
    rows = lambda s: slice(s * SUPER, (s + 1) * SUPER)
    q = [q_all[h][rows(s)] for h, s in units]
    k = [k_all[h][rows(s)] for h, s in units]
    v = [v_all[h][rows(s)] for h, s in units]
    g_row = [gc_ref[h, :, rows(s)] for h, s in units]
    gdk_row = [gdk_ref[h, :, rows(s)] for h, s in units]
    g_col = [_col_bcast(g) for g in g_row]
    gdk_col = [_col_bcast(g) for g in gdk_row]
    beta_col = [_col_bcast(beta_ref[h, :, rows(s)]) for h, s in units]
    decay = [jnp.exp(jnp.where(causal, gc - jnp.broadcast_to(gr, (SUPER, SUPER)), -jnp.inf))
             for gc, gr in zip(g_col, g_row)]
    exp_g = [jnp.exp(g) for g in g_col]
    g_last = [jnp.broadcast_to(jnp.exp(gr + gd), (SUPER, SUPER)) for gr, gd in zip(g_row, gdk_row)]

    kb = [x.astype(BF16) for x in k]
    kk = [_dot_nt(x, x) for x in kb]
    qk = [_dot_nt(a.astype(BF16), b) for a, b in zip(q, kb)]
    nmat = [jnp.where(strict, b * x * d, 0.0) for b, x, d in zip(beta_col, kk, decay)]
    tinv = _unit_lower_inverses(nmat, ri, ci)
    rhs = [jnp.concatenate([kx * b * e, vx * b], axis=1).astype(BF16)
           for kx, vx, b, e in zip(k, v, beta_col, exp_g)]
    wu = [_dot(x.astype(BF16), r).astype(BF16) for x, r in zip(tinv, rhs)]
    attn = [(a * d).astype(BF16) for a, d in zip(qk, decay)]
    a_wu = [_dot(a, x) for a, x in zip(attn, wu)]
    q_eff = [(qx * e - aw[:, :HEAD_DIM]).astype(BF16) for qx, e, aw in zip(q, exp_g, a_wu)]
    o_intra = [aw[:, HEAD_DIM:] for aw in a_wu]
    kd_t = [(kx * jnp.exp(g)).T for kx, g in zip(k, gdk_col)]
    in_chunk = [_block_id(ci, CHUNK) == c for c in range(n_chunk)]
    pr = [[_dot(jnp.where(in_chunk[c], kt, 0.0).astype(BF16), x) for c in range(n_chunk)]
          for kt, x in zip(kd_t, wu)]

    outs = [[] for _ in heads]
    for s in range(n_super):
        for c in range(n_chunk):
            cr = slice(c * CHUNK, (c + 1) * CHUNK)
            for h in heads:
                u = s * hb + h
                lhs = jnp.concatenate([q_eff[u][cr], pr[u][c][:, :HEAD_DIM].astype(BF16)], axis=0)
                r = _dot(lhs, states[h].astype(BF16))
                outs[h].append(r[:CHUNK] + o_intra[u][cr])
                gl = jnp.broadcast_to(g_last[u][:, c * CHUNK:c * CHUNK + 1], (HEAD_DIM, HEAD_DIM))
                states[h] = states[h] * gl + (pr[u][c][:, HEAD_DIM:] - r[CHUNK:])

    o_norm = on_ref[...]
    for h in heads:
        o = jnp.concatenate(outs[h], axis=0)
        o = o * _rms_scale(o) * o_norm * _silu(z_ref[:, lanes[h]].astype(F32))
        o_ref[:, lanes[h]] = o.astype(o_ref.dtype)
    for h in heads:
        state_ref[h] = states[h]


def delta_core(proj, conv_w, gc, beta, gdk, o_norm, *, tc, hb):
    s = proj.shape[0]
    nh = DN_HEADS
    assert nh % hb == 0
    ng = nh // hb
    halo = V7X_BF16_SUBLANES
    nhalo = tc // halo
    width = hb * HEAD_DIM

    def cur(sec):
        return pl.BlockSpec((tc, width), lambda g, t: (t, sec * ng + g))

    def prev(sec):
        return pl.BlockSpec((halo, width), lambda g, t: (jnp.maximum(t * nhalo - 1, 0), sec * ng + g))

    def cw(sec):
        return pl.BlockSpec((CONV_K, width), lambda g, t: (0, sec * ng + g))

    gate = pl.BlockSpec((hb, 1, tc), lambda g, t: (g, 0, t))
    g3 = lambda a: a.reshape(GATE_ROWS, 1, s)
    return pl.pallas_call(
        functools.partial(_delta_body, tc=tc, hb=hb),
        grid=(ng, s // tc),
        in_specs=[cur(0), cur(1), cur(2), cur(3), prev(0), prev(1), prev(2),
                  cw(0), cw(1), cw(2), gate, gate, gate,
                  pl.BlockSpec((1, HEAD_DIM), lambda g, t: (0, 0))],
        out_specs=pl.BlockSpec((tc, width), lambda g, t: (t, g)),
        out_shape=jax.ShapeDtypeStruct((s, nh * HEAD_DIM), BF16),
        scratch_shapes=[pltpu.VMEM((hb, HEAD_DIM, HEAD_DIM), F32),
                        pltpu.VMEM((3 * hb, tc + halo, HEAD_DIM), F32)],
        compiler_params=_cparams(("parallel", "arbitrary")),
        name="delta_core",
    )(proj, proj, proj, proj, proj, proj, proj, conv_w, conv_w, conv_w,
      g3(gc), g3(beta), g3(gdk), o_norm.reshape(1, HEAD_DIM))


def _dilated_body(*refs, dils):
    ng = len(dils)
    q_refs, kc_refs, kp_refs, vc_refs, vp_refs, slope_refs = (refs[i * ng:(i + 1) * ng] for i in range(6))
    o_ref, qf, kf, vf, of, lf = refs[6 * ng:]
    n = pl.program_id(0)
    blk = DIL_BLOCK
    tt = o_ref.shape[0]
    ri = lax.broadcasted_iota(jnp.int32, (blk, 2 * blk), 0)
    ci = lax.broadcasted_iota(jnp.int32, (blk, 2 * blk), 1)
    rel = ri + blk - ci
    window = (rel >= 0) & (rel <= blk)
    window_first = window & ((ci >= blk) | (n > 0))
    rel_f = rel.astype(F32)

    for g, dil in enumerate(dils):
        span = blk * dil
        qf[...] = q_refs[g][...].astype(F32)
        kf[0:span, :] = kp_refs[g][...].astype(F32)
        kf[span:span + tt, :] = kc_refs[g][...].astype(F32)
        vf[0:span, :] = vp_refs[g][...].astype(F32)
        vf[span:span + tt, :] = vc_refs[g][...].astype(F32)
        bias = (slope_refs[g][:, 0:1] * (-float(dil))) * rel_f

        def rows(start, size, dil=dil):
            return pl.ds(start, size, stride=dil) if dil > 1 else pl.ds(start, size)

        units = [(b, c + span * b) for b in range(tt // span) for c in range(dil)]
        qs = [qf[rows(st, blk), :].astype(BF16) for _, st in units]
        ks = [kf[rows(st, 2 * blk), :].astype(BF16) for _, st in units]
        vs = [vf[rows(st, 2 * blk), :].astype(BF16) for _, st in units]
        s = [_dot_nt(a, b_) * (HEAD_DIM ** -0.5) + bias for a, b_ in zip(qs, ks)]
        s = [jnp.where(window_first if b == 0 else window, x, -jnp.inf) for (b, _), x in zip(units, s)]
        mx = [jnp.max(x, axis=-1, keepdims=True) for x in s]
        e = [jnp.exp(x - m) for x, m in zip(s, mx)]
        den = [jnp.sum(x, axis=-1, keepdims=True) for x in e]
        o = [_dot(x.astype(BF16), v) / d for x, v, d in zip(e, vs, den)]
        for (_, st), ou, m, d in zip(units, o, mx, den):
            of[g, rows(st, blk), :] = ou
            lf[g, rows(st, blk), :] = jnp.broadcast_to(m + jnp.log(d), (blk, HEAD_DIM))

    lses = [lf[g] for g in range(ng)]
    top = functools.reduce(jnp.maximum, lses)
    wts = [jnp.exp(x - top) for x in lses]
    num = sum(w * of[g] for g, w in enumerate(wts))
    o_ref[...] = (num / sum(wts)).astype(o_ref.dtype)


def dilated_attention(qproj, kv, *, tt):
    s = qproj.shape[0]
    dils = tuple(dil for _, dil in DIL_GROUPS)
    assert all(win // dil == DIL_BLOCK for win, dil in DIL_GROUPS)
    assert all(tt % (DIL_BLOCK * dil) == 0 for dil in dils) and s % tt == 0
    hpg = DIL_HEADS_PER_GROUP
    slopes = jnp.asarray([2.0 ** (-8.0 * (h + 1) / DIL_HEADS) for h in range(DIL_HEADS)], F32)
    slopes = jnp.broadcast_to(slopes.reshape(len(dils), hpg, 1, 1), (len(dils), hpg, 1, HEAD_DIM))

    def cur(col0):
        return [pl.BlockSpec((tt, HEAD_DIM), lambda n, h, g=g: (n, col0 + g * hpg + h)) for g in range(len(dils))]

    def prev(col0):
        specs = []
        for g, dil in enumerate(dils):
            span = DIL_BLOCK * dil
            specs.append(pl.BlockSpec(
                (span, HEAD_DIM),
                lambda n, h, g=g, per=tt // span: (jnp.maximum(n * per - 1, 0), col0 + g * hpg + h)))
        return specs

    slope_specs = [pl.BlockSpec((None, None, 1, HEAD_DIM), lambda n, h, g=g: (g, h, 0, 0)) for g in range(len(dils))]
    ng = len(dils)
    max_span = DIL_BLOCK * max(dils)
    return pl.pallas_call(
        functools.partial(_dilated_body, dils=dils),
        grid=(s // tt, hpg),
        in_specs=cur(0) + cur(0) + prev(0) + cur(DIL_HEADS) + prev(DIL_HEADS) + slope_specs,
        out_specs=pl.BlockSpec((tt, HEAD_DIM), lambda n, h: (n, h)),
        out_shape=jax.ShapeDtypeStruct((s, hpg * HEAD_DIM), BF16),
        scratch_shapes=[pltpu.VMEM((tt, HEAD_DIM), F32),
                        pltpu.VMEM((max_span + tt, HEAD_DIM), F32),
                        pltpu.VMEM((max_span + tt, HEAD_DIM), F32),
                        pltpu.VMEM((ng, tt, HEAD_DIM), F32),
                        pltpu.VMEM((ng, tt, HEAD_DIM), F32)],
        compiler_params=_cparams(("parallel", "arbitrary")),
        name="dilated_attention",
    )(*([qproj] * ng + [kv] * (4 * ng) + [slopes] * ng))


def _deltanet_layer(x, g_pre, g_post, mem_kv, dn_w_in, l, conv_w, a_log, dt_bias, o_norm, dn_w_out):
    qkvz = 4 * DN_WIDTH
    w_in_t = jnp.swapaxes(dn_w_in, 1, 2)
    tail = lax.optimization_barrier(w_in_t[l, qkvz:])
    w_mq_t = tail[2 * DN_HEADS:].astype(BF16)
    gate_w = lambda lo: jnp.pad(tail[lo:lo + DN_HEADS], ((0, GATE_ROWS - DN_HEADS), (0, 0)))
    w_ab_t = jnp.concatenate([gate_w(0), gate_w(DN_HEADS)], axis=0).astype(BF16)
    proj, mq, ab_t = dn_in_proj(x, g_pre, w_in_t, (l,), w_mq_t, w_ab_t, tm=1024, tn=1024)
    gc, beta, gdk = dn_gates(ab_t, a_log, dt_bias, tl=1024)
    o = delta_core(proj, conv_w.astype(F32), gc, beta, gdk, o_norm, tc=256, hb=12)
    return mixer_out_proj(o, mq, 0, mem_kv, dn_w_out[l].astype(BF16), x, g_post, tm=512)


def _dilated_layer(x, g_pre, g_post, kv, mem_kv, dil_w_in, i, dil_w_out):
    proj = prenorm_matmul(x, g_pre, dil_w_in, (i,), tm=1024, tn=1024)
    o = dilated_attention(proj, kv, tt=2048)
    return mixer_out_proj(o, proj, DIL_WIDTH // MEM_WIDTH, mem_kv, dil_w_out[i].astype(BF16), x, g_post, tm=512)


def kernel(x, mem, norm_gains, ffn_w_in, ffn_w_out, mem_norm_gain, w_mem_kv, dn_w_in, dn_conv, dn_a_log,
           dn_dt_bias, dn_o_norm, dn_w_out, kv_norm_gain, w_kv, dil_w_in, dil_w_out):
    batch, seq, d = x.shape
    assert batch == 1
    depth = norm_gains.shape[0]
    n_a = dn_w_in.shape[0]
    xs = x.reshape(seq, d)
    mem2 = mem.reshape(mem.shape[1], d)
    mem_kv_all = memory_kv_all_layers(mem2, mem_norm_gain, w_mem_kv, tn=512)
    kv = None
    for l in range(depth):
        gains = norm_gains[l]
        if l == n_a:
            kv = prenorm_matmul(xs, kv_norm_gain, w_kv, tm=1024, tn=1024)
        xs = ffn(xs, gains[0], gains[1], ffn_w_in, ffn_w_out, (l, 0), tm=1024)
        mem_kv = mem_kv_all[l]
        if l < n_a:
            xs = _deltanet_layer(xs, gains[2], gains[3], mem_kv, dn_w_in, l, dn_conv[l], dn_a_log[l],
                                 dn_dt_bias[l], dn_o_norm[l], dn_w_out)
        else:
            i = l - n_a
            xs = _dilated_layer(xs, gains[2], gains[3], kv, mem_kv, dil_w_in, i, dil_w_out)
        xs = ffn(xs, gains[4], gains[5], ffn_w_in, ffn_w_out, (l, 1), tm=1024)
    return xs.reshape(batch, seq, d)
```

```python
import functools

import jax
import jax.numpy as jnp
from jax import lax
from jax.experimental import pallas as pl
from jax.experimental.pallas import tpu as pltpu

F32 = jnp.float32
BF16 = jnp.bfloat16

HEAD_DIM = 128
DN_HEADS = 12
DN_WIDTH = DN_HEADS * HEAD_DIM
CONV_K = 4
CHUNK = 64
MEM_HEADS = 4
MEM_WIDTH = MEM_HEADS * HEAD_DIM
DIL_GROUPS = ((128, 1), (512, 4), (2048, 16))
DIL_HEADS_PER_GROUP = 4
DIL_HEADS = DIL_HEADS_PER_GROUP * len(DIL_GROUPS)
DIL_WIDTH = DIL_HEADS * HEAD_DIM
DIL_BLOCK = 128
NORM_EPS = 1e-6

V7X_LANES = 128
V7X_BF16_SUBLANES = 16
V7X_VMEM_LIMIT_BYTES = 56 * 1024 * 1024

GATE_ROWS = 16
SUPER = 2 * CHUNK


def _cparams(sem):
    return pltpu.CompilerParams(dimension_semantics=sem, vmem_limit_bytes=V7X_VMEM_LIMIT_BYTES)


def _rms_scale(x):
    return lax.rsqrt(jnp.mean(x * x, axis=-1, keepdims=True) + NORM_EPS)


def _silu(x):
    return x * jax.nn.sigmoid(x)


def _block_id(idx, size):
    assert size & (size - 1) == 0
    return lax.shift_right_logical(idx, size.bit_length() - 1)


def _dot(a, b):
    return jnp.dot(a, b, preferred_element_type=F32)


def _dot_nt(a, b):
    return lax.dot_general(a, b, (((1,), (1,)), ((), ())), preferred_element_type=F32)


NORM_ROW_GROUP = 256


def _row_groups(tm):
    step = min(tm, NORM_ROW_GROUP)
    return [slice(r, r + step) for r in range(0, tm, step)]


def _prenorm_matmul_body(x_ref, g_ref, w_ref, o_ref, hn_ref):
    j = pl.program_id(1)

    @pl.when(j == 0)
    def _():
        w = w_ref[...].astype(BF16)
        for rows in _row_groups(x_ref.shape[0]):
            x = x_ref[rows, :]
            hn = (x * _rms_scale(x) * g_ref[...]).astype(BF16)
            hn_ref[rows, :] = hn
            o_ref[rows, :] = _dot(hn, w).astype(o_ref.dtype)

    @pl.when(j > 0)
    def _():
        o_ref[...] = _dot(hn_ref[...], w_ref[...].astype(BF16)).astype(o_ref.dtype)


def _wspec(lead, block, index_fn):
    return pl.BlockSpec((None,) * len(lead) + block, lambda *g: lead + index_fn(*g))


def prenorm_matmul(x, gain, w, lead=(), *, tm, tn, out_dtype=BF16):
    m, d = x.shape
    n = w.shape[-1]
    return pl.pallas_call(
        _prenorm_matmul_body,
        grid=(m // tm, n // tn),
        in_specs=[
            pl.BlockSpec((tm, d), lambda i, j: (i, 0)),
            pl.BlockSpec((1, d), lambda i, j: (0, 0)),
            _wspec(lead, (d, tn), lambda i, j: (0, j)),
        ],
        out_specs=pl.BlockSpec((tm, tn), lambda i, j: (i, j)),
        out_shape=jax.ShapeDtypeStruct((m, n), out_dtype),
        scratch_shapes=[pltpu.VMEM((tm, d), BF16)],
        compiler_params=_cparams(("parallel", "arbitrary")),
        name="prenorm_matmul",
    )(x, gain.reshape(1, d), w)


def memory_kv_all_layers(mem, gains, w, *, tn):
    m, d = mem.shape
    nl, _, n = w.shape
    return pl.pallas_call(
        _prenorm_matmul_body,
        grid=(nl, n // tn),
        in_specs=[
            pl.BlockSpec((m, d), lambda l, j: (0, 0)),
            pl.BlockSpec((None, 1, d), lambda l, j: (l, 0, 0)),
            pl.BlockSpec((None, d, tn), lambda l, j: (l, 0, j)),
        ],
        out_specs=pl.BlockSpec((None, m, tn), lambda l, j: (l, 0, j)),
        out_shape=jax.ShapeDtypeStruct((nl, m, n), BF16),
        scratch_shapes=[pltpu.VMEM((m, d), BF16)],
        compiler_params=_cparams(("parallel", "arbitrary")),
        name="memory_kv",
    )(mem, gains.reshape(nl, 1, d), w)


def _dn_in_proj_body(x_ref, g_ref, w_ref, wmq_ref, wab_ref, o_ref, mq_ref, ab_ref, hn_ref):
    j = pl.program_id(1)

    @pl.when(j == 0)
    def _():
        w = w_ref[...].astype(BF16)
        for rows in _row_groups(x_ref.shape[0]):
            x = x_ref[rows, :]
            hn = (x * _rms_scale(x) * g_ref[...]).astype(BF16)
            hn_ref[rows, :] = hn
            ab_ref[:, rows] = _dot_nt(wab_ref[...], hn)
            mq_ref[rows, :] = _dot_nt(hn, wmq_ref[...]).astype(mq_ref.dtype)
            o_ref[rows, :] = _dot_nt(hn, w).astype(o_ref.dtype)

    @pl.when(j > 0)
    def _():
        o_ref[...] = _dot_nt(hn_ref[...], w_ref[...].astype(BF16)).astype(o_ref.dtype)


def dn_in_proj(x, gain, w_in_t, lead, w_mq_t, w_ab_t, *, tm, tn):
    m, d = x.shape
    n_main = (4 * DN_WIDTH) // tn
    assert n_main * tn == 4 * DN_WIDTH
    nq = w_mq_t.shape[0]
    r = w_ab_t.shape[0]
    return pl.pallas_call(
        _dn_in_proj_body,
        grid=(m // tm, n_main),
        in_specs=[
            pl.BlockSpec((tm, d), lambda i, j: (i, 0)),
            pl.BlockSpec((1, d), lambda i, j: (0, 0)),
            _wspec(lead, (tn, d), lambda i, j: (j, 0)),
            pl.BlockSpec((nq, d), lambda i, j: (0, 0)),
            pl.BlockSpec((r, d), lambda i, j: (0, 0)),
        ],
        out_specs=[
            pl.BlockSpec((tm, tn), lambda i, j: (i, j)),
            pl.BlockSpec((tm, nq), lambda i, j: (i, 0)),
            pl.BlockSpec((r, tm), lambda i, j: (0, i)),
        ],
        out_shape=[
            jax.ShapeDtypeStruct((m, n_main * tn), BF16),
            jax.ShapeDtypeStruct((m, nq), BF16),
            jax.ShapeDtypeStruct((r, m), F32),
        ],
        scratch_shapes=[pltpu.VMEM((tm, d), BF16)],
        compiler_params=_cparams(("parallel", "arbitrary")),
        name="dn_in_proj",
    )(x, gain.reshape(1, d), w_in_t, w_mq_t, w_ab_t)


def _postnorm_residual(x, y, gain, scale):
    return x + scale * (y * _rms_scale(y) * gain)


FFN_ROW_GROUP = 256


def _ffn_body(x_ref, gpre_ref, ga_ref, gb_ref, ua_ref, ub_ref, wa_ref, wb_ref, gpost_ref, o_ref, hn_ref,
              *, odd_tail):
    j = pl.program_id(1)
    last = pl.num_programs(1) - 1
    tm = x_ref.shape[0]
    groups = [slice(r, r + FFN_ROW_GROUP) for r in range(0, tm, FFN_ROW_GROUP)]

    def weights(zero_tail):
        w_gu = jnp.concatenate([ga_ref[...], gb_ref[...], ua_ref[...], ub_ref[...]], axis=1).astype(BF16)
        wb = jnp.zeros(wb_ref.shape, wb_ref.dtype) if zero_tail else wb_ref[...]
        return w_gu, jnp.concatenate([wa_ref[...], wb], axis=0).astype(BF16)

    def down(hn, w_gu, w_o):
        gu = _dot(hn, w_gu)
        half = gu.shape[1] // 2
        return _dot((_silu(gu[:, :half]) * gu[:, half:]).astype(BF16), w_o)

    @pl.when(j == 0)
    def _():
        w = weights(False)
        for rows in groups:
            x = x_ref[rows, :]
            hn = (x * _rms_scale(x) * gpre_ref[...]).astype(BF16)
            hn_ref[rows, :] = hn
            o_ref[rows, :] = down(hn, *w)

    @pl.when((j > 0) & (j < last))
    def _():
        o_ref[...] += down(hn_ref[...], *weights(False))

    @pl.when(j == last)
    def _():
        w = weights(odd_tail)
        for rows in groups:
            y = o_ref[rows, :] + down(hn_ref[rows, :], *w)
            o_ref[rows, :] = _postnorm_residual(x_ref[rows, :], y, gpost_ref[...], 0.5)


def ffn(x, g_pre, g_post, w_in, w_out, lead, *, tm):
    m, d = x.shape
    ff = w_out.shape[-2]
    blk = V7X_LANES
    nblk = ff // blk
    assert nblk * blk == ff and w_in.shape[-2:] == (d, 2 * ff)
    nf = -(-nblk // 2)
    assert nf >= 2 and tm % FFN_ROW_GROUP == 0
    second = lambda j: jnp.minimum(2 * j + 1, nblk - 1)
    col = lambda f: _wspec(lead, (d, blk), lambda i, j: (0, f(j)))
    row = lambda f: _wspec(lead, (blk, d), lambda i, j: (f(j), 0))
    vec = pl.BlockSpec((1, d), lambda i, j: (0, 0))
    return pl.pallas_call(
        functools.partial(_ffn_body, odd_tail=bool(nblk % 2)),
        grid=(m // tm, nf),
        in_specs=[
            pl.BlockSpec((tm, d), lambda i, j: (i, 0)),
            vec,
            col(lambda j: 2 * j), col(second),
            col(lambda j: nblk + 2 * j), col(lambda j: nblk + second(j)),
            row(lambda j: 2 * j), row(second),
            vec,
        ],
        out_specs=pl.BlockSpec((tm, d), lambda i, j: (i, 0)),
        out_shape=jax.ShapeDtypeStruct((m, d), F32),
        scratch_shapes=[pltpu.VMEM((tm, d), BF16)],
        compiler_params=_cparams(("parallel", "arbitrary")),
        name="ffn",
    )(x, g_pre.reshape(1, d), w_in, w_in, w_in, w_in, w_out, w_out, g_post.reshape(1, d))


def _mixer_out_body(o_ref, mq_ref, mk_ref, mv_ref, w1_ref, w2_ref, x_ref, g_ref, out_ref):
    heads = []
    for h in range(MEM_HEADS):
        cs = slice(h * HEAD_DIM, (h + 1) * HEAD_DIM)
        s = _dot_nt(mq_ref[:, cs], mk_ref[:, cs]) * (HEAD_DIM ** -0.5)
        e = jnp.exp(s - jnp.max(s, axis=-1, keepdims=True))
        den = jnp.sum(e, axis=-1, keepdims=True)
        heads.append((_dot(e.astype(BF16), mv_ref[:, cs]) / den).astype(BF16))
    o_mem = jnp.concatenate(heads, axis=1)
    y = _dot(o_ref[...], w1_ref[...]) + _dot(o_mem, w2_ref[...])
    out_ref[...] = _postnorm_residual(x_ref[...], y, g_ref[...], 1.0)


def mixer_out_proj(o, mq_src, mq_col_block, mem_kv, w_out, x, gain, *, tm):
    m, d = x.shape
    k1, k2 = o.shape[1], MEM_WIDTH
    mlen = mem_kv.shape[0]
    assert k1 % k2 == 0 and w_out.shape == (k1 + k2, d)
    return pl.pallas_call(
        _mixer_out_body,
        grid=(m // tm,),
        in_specs=[
            pl.BlockSpec((tm, k1), lambda i: (i, 0)),
            pl.BlockSpec((tm, k2), lambda i: (i, mq_col_block)),
            pl.BlockSpec((mlen, k2), lambda i: (0, 0)),
            pl.BlockSpec((mlen, k2), lambda i: (0, 1)),
            pl.BlockSpec((k1, d), lambda i: (0, 0)),
            pl.BlockSpec((k2, d), lambda i: (k1 // k2, 0)),
            pl.BlockSpec((tm, d), lambda i: (i, 0)),
            pl.BlockSpec((1, d), lambda i: (0, 0)),
        ],
        out_specs=pl.BlockSpec((tm, d), lambda i: (i, 0)),
        out_shape=jax.ShapeDtypeStruct((m, d), F32),
        compiler_params=_cparams(("parallel",)),
        name="mixer_out_proj",
    )(o, mq_src, mem_kv, mem_kv, w_out, w_out, x, gain.reshape(1, d))


def _gate_body(ab_ref, alog_ref, dtb_ref, gc_ref, beta_ref, gdk_ref):
    tl = ab_ref.shape[1]
    a = ab_ref[0:GATE_ROWS, :]
    b = ab_ref[GATE_ROWS:2 * GATE_ROWS, :]
    z = a + dtb_ref[...]
    softplus = jnp.maximum(z, 0.0) + jnp.log1p(jnp.exp(-jnp.abs(z)))
    g = -jnp.exp(alog_ref[...]) * softplus
    beta_ref[...] = jax.nn.sigmoid(b)
    r = lax.broadcasted_iota(jnp.int32, (V7X_LANES, V7X_LANES), 0)
    c = lax.broadcasted_iota(jnp.int32, (V7X_LANES, V7X_LANES), 1)
    same = _block_id(r, CHUNK) == _block_id(c, CHUNK)
    upper = jnp.where(same & (r <= c), 1.0, 0.0).astype(F32)
    ones = jnp.where(same, 1.0, 0.0).astype(F32)
    for s in range(tl // V7X_LANES):
        ls = slice(s * V7X_LANES, (s + 1) * V7X_LANES)
        gs = g[:, ls]
        gc = jnp.dot(gs, upper, preferred_element_type=F32, precision=lax.Precision.HIGHEST)
        gend = jnp.dot(gs, ones, preferred_element_type=F32, precision=lax.Precision.HIGHEST)
        gc_ref[:, ls] = gc
        gdk_ref[:, ls] = gend - gc


def dn_gates(ab_t, a_log, dt_bias, *, tl):
    r, s = ab_t.shape
    pad = lambda v: jnp.pad(v.astype(F32), (0, GATE_ROWS - v.shape[0])).reshape(GATE_ROWS, 1)
    out = jax.ShapeDtypeStruct((GATE_ROWS, s), F32)
    return pl.pallas_call(
        _gate_body,
        grid=(s // tl,),
        in_specs=[
            pl.BlockSpec((r, tl), lambda i: (0, i)),
            pl.BlockSpec((GATE_ROWS, 1), lambda i: (0, 0)),
            pl.BlockSpec((GATE_ROWS, 1), lambda i: (0, 0)),
        ],
        out_specs=[pl.BlockSpec((GATE_ROWS, tl), lambda i: (0, i))] * 3,
        out_shape=[out, out, out],
        compiler_params=_cparams(("parallel",)),
        name="dn_gates",
    )(ab_t, pad(a_log), pad(dt_bias))


def _col_bcast(row):
    n = row.shape[1]
    return jnp.broadcast_to(row, (n, n)).T


def _unit_lower_inverses(nmats, ri, ci):
    def quarter(b):
        same_block = _block_id(ri, 2 * b) == _block_id(ci, 2 * b)
        return same_block & ((ri & (2 * b - 1)) >= b) & ((ci & (2 * b - 1)) < b)

    eye = jnp.where(ri == ci, 1.0, 0.0).astype(F32)
    first = quarter(1)
    xs = [eye - jnp.where(first, n, 0.0) for n in nmats]
    b = 2
    while b < CHUNK:
        mask = quarter(b)
        ys = [_dot(jnp.where(mask, n, 0.0).astype(BF16), x.astype(BF16)) for n, x in zip(nmats, xs)]
        xs = [x - _dot(x.astype(BF16), y.astype(BF16)) for x, y in zip(xs, ys)]
        b *= 2
    return xs


def _delta_body(q_ref, k_ref, v_ref, z_ref, qh_ref, kh_ref, vh_ref, wq_ref, wk_ref, wv_ref,
                gc_ref, beta_ref, gdk_ref, on_ref, o_ref, state_ref, buf_ref, *, tc, hb):
    t = pl.program_id(1)
    halo = V7X_BF16_SUBLANES
    n_super = tc // SUPER
    n_chunk = SUPER // CHUNK

    @pl.when(t == 0)
    def _():
        state_ref[...] = jnp.zeros_like(state_ref)

    states = [state_ref[h] for h in range(hb)]

    ri = lax.broadcasted_iota(jnp.int32, (SUPER, SUPER), 0)
    ci = lax.broadcasted_iota(jnp.int32, (SUPER, SUPER), 1)
    same = _block_id(ri, CHUNK) == _block_id(ci, CHUNK)
    causal = same & (ri >= ci)
    strict = same & (ri > ci)

    def conv_silu(slot, cur_ref, halo_ref, w_ref, cs):
        buf_ref[slot, 0:halo, :] = jnp.where(t == 0, 0.0, halo_ref[:, cs].astype(F32))
        buf_ref[slot, halo:halo + tc, :] = cur_ref[:, cs].astype(F32)
        y = jnp.zeros((tc, HEAD_DIM), F32)
        for j in range(CONV_K):
            off = halo - (CONV_K - 1) + j
            y = y + w_ref[j:j + 1, cs] * buf_ref[slot, off:off + tc, :]
        return _silu(y)

    def l2n(x):
        return x * lax.rsqrt(jnp.sum(x * x, axis=-1, keepdims=True) + NORM_EPS)

    heads = range(hb)
    lanes = [slice(h * HEAD_DIM, (h + 1) * HEAD_DIM) for h in heads]
    q_all = [l2n(conv_silu(3 * h, q_ref, qh_ref, wq_ref, lanes[h])) * (HEAD_DIM ** -0.5) for h in heads]
    k_all = [l2n(conv_silu(3 * h + 1, k_ref, kh_ref, wk_ref, lanes[h])) for h in heads]
    v_all = [conv_silu(3 * h + 2, v_ref, vh_ref, wv_ref, lanes[h]) for h in heads]

    units = [(h, s) for s in range(n_super) for h in heads]
    rows = lambda s: slice(s * SUPER, (s + 1) * SUPER)
    q = [q_all[h][rows(s)] for h, s in units]
    k = [k_all[h][rows(s)] for h, s in units]
    v = [v_all[h][rows(s)] for h, s in units]
    g_row = [gc_ref[h, :, rows(s)] for h, s in units]
    gdk_row = [gdk_ref[h, :, rows(s)] for h, s in units]
    g_col = [_col_bcast(g) for g in g_row]
    gdk_col = [_col_bcast(g) for g in gdk_row]
    beta_col = [_col_bcast(beta_ref[h, :, rows(s)]) for h, s in units]
    decay = [jnp.exp(jnp.where(causal, gc - jnp.broadcast_to(gr, (SUPER, SUPER)), -jnp.inf))
             for gc, gr in zip(g_col, g_row)]
    exp_g = [jnp.exp(g) for g in g_col]
    g_last = [jnp.broadcast_to(jnp.exp(gr + gd), (SUPER, SUPER)) for gr, gd in zip(g_row, gdk_row)]

    kb = [x.astype(BF16) for x in k]
    kq_k = [_dot_nt(jnp.concatenate([b, a.astype(BF16)], axis=0), b) for a, b in zip(q, kb)]
    kk = [x[:SUPER] for x in kq_k]
    qk = [x[SUPER:] for x in kq_k]
    nmat = [jnp.where(strict, b * x * d, 0.0) for b, x, d in zip(beta_col, kk, decay)]
    tinv = _unit_lower_inverses(nmat, ri, ci)
    rhs = [jnp.concatenate([kx * b * e, vx * b], axis=1).astype(BF16)
           for kx, vx, b, e in zip(k, v, beta_col, exp_g)]
    wu = [_dot(x.astype(BF16), r).astype(BF16) for x, r in zip(tinv, rhs)]
    attn = [(a * d).astype(BF16) for a, d in zip(qk, decay)]
    a_wu = [_dot(a, x) for a, x in zip(attn, wu)]
    q_eff = [(qx * e - aw[:, :HEAD_DIM]).astype(BF16) for qx, e, aw in zip(q, exp_g, a_wu)]
    o_intra = [aw[:, HEAD_DIM:] for aw in a_wu]
    kd_t = [(kx * jnp.exp(g)).T for kx, g in zip(k, gdk_col)]
    in_chunk = [_block_id(ci, CHUNK) == c for c in range(n_chunk)]
    pr = [[_dot(jnp.where(in_chunk[c], kt, 0.0).astype(BF16), x) for c in range(n_chunk)]
          for kt, x in zip(kd_t, wu)]

    outs = [[] for _ in heads]
    for s in range(n_super):
        for c in range(n_chunk):
            cr = slice(c * CHUNK, (c + 1) * CHUNK)
            for h in heads:
                u = s * hb + h
                lhs = jnp.concatenate([q_eff[u][cr], pr[u][c][:, :HEAD_DIM].astype(BF16)], axis=0)
                r = _dot(lhs, states[h].astype(BF16))
                outs[h].append(r[:CHUNK] + o_intra[u][cr])
                gl = jnp.broadcast_to(g_last[u][:, c * CHUNK:c * CHUNK + 1], (HEAD_DIM, HEAD_DIM))
                states[h] = states[h] * gl + (pr[u][c][:, HEAD_DIM:] - r[CHUNK:])

    o_norm = on_ref[...]
    for h in heads:
        o = jnp.concatenate(outs[h], axis=0)
        o = o * _rms_scale(o) * o_norm * _silu(z_ref[:, lanes[h]].astype(F32))
        o_ref[:, lanes[h]] = o.astype(o_ref.dtype)
    for h in heads:
        state_ref[h] = states[h]


def delta_core(proj, conv_w, gc, beta, gdk, o_norm, *, tc, hb):
    s = proj.shape[0]
    nh = DN_HEADS
    assert nh % hb == 0
    ng = nh // hb
    halo = V7X_BF16_SUBLANES
    nhalo = tc // halo
    width = hb * HEAD_DIM

    def cur(sec):
        return pl.BlockSpec((tc, width), lambda g, t: (t, sec * ng + g))

    def prev(sec):
        return pl.BlockSpec((halo, width), lambda g, t: (jnp.maximum(t * nhalo - 1, 0), sec * ng + g))

    def cw(sec):
        return pl.BlockSpec((CONV_K, width), lambda g, t: (0, sec * ng + g))

    gate = pl.BlockSpec((hb, 1, tc), lambda g, t: (g, 0, t))
    g3 = lambda a: a.reshape(GATE_ROWS, 1, s)
    return pl.pallas_call(
        functools.partial(_delta_body, tc=tc, hb=hb),
        grid=(ng, s // tc),
        in_specs=[cur(0), cur(1), cur(2), cur(3), prev(0), prev(1), prev(2),
                  cw(0), cw(1), cw(2), gate, gate, gate,
                  pl.BlockSpec((1, HEAD_DIM), lambda g, t: (0, 0))],
        out_specs=pl.BlockSpec((tc, width), lambda g, t: (t, g)),
        out_shape=jax.ShapeDtypeStruct((s, nh * HEAD_DIM), BF16),
        scratch_shapes=[pltpu.VMEM((hb, HEAD_DIM, HEAD_DIM), F32),
                        pltpu.VMEM((3 * hb, tc + halo, HEAD_DIM), F32)],
        compiler_params=_cparams(("parallel", "arbitrary")),
        name="delta_core",
    )(proj, proj, proj, proj, proj, proj, proj, conv_w, conv_w, conv_w,
      g3(gc), g3(beta), g3(gdk), o_norm.reshape(1, HEAD_DIM))


def _dilated_body(*refs, dils):
    ng = len(dils)
    q_refs, kc_refs, kp_refs, vc_refs, vp_refs, slope_refs = (refs[i * ng:(i + 1) * ng] for i in range(6))
    o_ref, qf, kf, vf, of, lf = refs[6 * ng:]
    n = pl.program_id(0)
    blk = DIL_BLOCK
    tt = o_ref.shape[0]
    ri = lax.broadcasted_iota(jnp.int32, (blk, 2 * blk), 0)
    ci = lax.broadcasted_iota(jnp.int32, (blk, 2 * blk), 1)
    rel = ri + blk - ci
    window = (rel >= 0) & (rel <= blk)
    window_first = window & ((ci >= blk) | (n > 0))
    rel_f = rel.astype(F32)

    for g, dil in enumerate(dils):
        span = blk * dil
        qf[...] = q_refs[g][...].astype(F32)
        kf[0:span, :] = kp_refs[g][...].astype(F32)
        kf[span:span + tt, :] = kc_refs[g][...].astype(F32)
        vf[0:span, :] = vp_refs[g][...].astype(F32)
        vf[span:span + tt, :] = vc_refs[g][...].astype(F32)
        bias = (slope_refs[g][:, 0:1] * (-float(dil))) * rel_f
        bias_in = jnp.where(window, bias, -jnp.inf)
        bias_first = jnp.where(window_first, bias, -jnp.inf)

        def rows(start, size, dil=dil):
            return pl.ds(start, size, stride=dil) if dil > 1 else pl.ds(start, size)

        units = [(b, c + span * b) for b in range(tt // span) for c in range(dil)]
        qs = [qf[rows(st, blk), :].astype(BF16) for _, st in units]
        ks = [kf[rows(st, 2 * blk), :].astype(BF16) for _, st in units]
        vs = [vf[rows(st, 2 * blk), :].astype(BF16) for _, st in units]
        s = [_dot_nt(a, b_) * (HEAD_DIM ** -0.5) + (bias_first if b == 0 else bias_in)
             for (b, _), a, b_ in zip(units, qs, ks)]
        mx = [jnp.max(x, axis=-1, keepdims=True) for x in s]
        e = [jnp.exp(x - m) for x, m in zip(s, mx)]
        den = [jnp.sum(x, axis=-1, keepdims=True) for x in e]
        o = [_dot(x.astype(BF16), v) / d for x, v, d in zip(e, vs, den)]
        for (_, st), ou, m, d in zip(units, o, mx, den):
            of[g, rows(st, blk), :] = ou
            lf[g, rows(st, blk), :] = jnp.broadcast_to(m + jnp.log(d), (blk, HEAD_DIM))

    lses = [lf[g] for g in range(ng)]
    top = functools.reduce(jnp.maximum, lses)
    wts = [jnp.exp(x - top) for x in lses]
    num = sum(w * of[g] for g, w in enumerate(wts))
    o_ref[...] = (num / sum(wts)).astype(o_ref.dtype)


def dilated_attention(qproj, kv, *, tt):
    s = qproj.shape[0]
    dils = tuple(dil for _, dil in DIL_GROUPS)
    assert all(win // dil == DIL_BLOCK for win, dil in DIL_GROUPS)
    assert all(tt % (DIL_BLOCK * dil) == 0 for dil in dils) and s % tt == 0
    hpg = DIL_HEADS_PER_GROUP
    slopes = jnp.asarray([2.0 ** (-8.0 * (h + 1) / DIL_HEADS) for h in range(DIL_HEADS)], F32)
    slopes = jnp.broadcast_to(slopes.reshape(len(dils), hpg, 1, 1), (len(dils), hpg, 1, HEAD_DIM))

    def cur(col0):
        return [pl.BlockSpec((tt, HEAD_DIM), lambda n, h, g=g: (n, col0 + g * hpg + h)) for g in range(len(dils))]

    def prev(col0):
        specs = []
        for g, dil in enumerate(dils):
            span = DIL_BLOCK * dil
            specs.append(pl.BlockSpec(
                (span, HEAD_DIM),
                lambda n, h, g=g, per=tt // span: (jnp.maximum(n * per - 1, 0), col0 + g * hpg + h)))
        return specs

    slope_specs = [pl.BlockSpec((None, None, 1, HEAD_DIM), lambda n, h, g=g: (g, h, 0, 0)) for g in range(len(dils))]
    ng = len(dils)
    max_span = DIL_BLOCK * max(dils)
    return pl.pallas_call(
        functools.partial(_dilated_body, dils=dils),
        grid=(s // tt, hpg),
        in_specs=cur(0) + cur(0) + prev(0) + cur(DIL_HEADS) + prev(DIL_HEADS) + slope_specs,
        out_specs=pl.BlockSpec((tt, HEAD_DIM), lambda n, h: (n, h)),
        out_shape=jax.ShapeDtypeStruct((s, hpg * HEAD_DIM), BF16),
        scratch_shapes=[pltpu.VMEM((tt, HEAD_DIM), F32),
                        pltpu.VMEM((max_span + tt, HEAD_DIM), F32),
                        pltpu.VMEM((max_span + tt, HEAD_DIM), F32),
                        pltpu.VMEM((ng, tt, HEAD_DIM), F32),
                        pltpu.VMEM((ng, tt, HEAD_DIM), F32)],
        compiler_params=_cparams(("parallel", "arbitrary")),
        name="dilated_attention",
    )(*([qproj] * ng + [kv] * (4 * ng) + [slopes] * ng))


def _deltanet_layer(x, g_pre, g_post, mem_kv, dn_w_in, l, conv_w, a_log, dt_bias, o_norm, dn_w_out):
    qkvz = 4 * DN_WIDTH
    w_in_t = jnp.swapaxes(dn_w_in, 1, 2)
    tail = lax.optimization_barrier(w_in_t[l, qkvz:])
    w_mq_t = tail[2 * DN_HEADS:].astype(BF16)
    gate_w = lambda lo: jnp.pad(tail[lo:lo + DN_HEADS], ((0, GATE_ROWS - DN_HEADS), (0, 0)))
    w_ab_t = jnp.concatenate([gate_w(0), gate_w(DN_HEADS)], axis=0).astype(BF16)
    proj, mq, ab_t = dn_in_proj(x, g_pre, w_in_t, (l,), w_mq_t, w_ab_t, tm=1024, tn=1024)
    gc, beta, gdk = dn_gates(ab_t, a_log, dt_bias, tl=1024)
    o = delta_core(proj, conv_w.astype(F32), gc, beta, gdk, o_norm, tc=256, hb=12)
    return mixer_out_proj(o, mq, 0, mem_kv, dn_w_out[l].astype(BF16), x, g_post, tm=512)


def _dilated_layer(x, g_pre, g_post, kv, mem_kv, dil_w_in, i, dil_w_out):
    proj = prenorm_matmul(x, g_pre, dil_w_in, (i,), tm=1024, tn=1024)
    o = dilated_attention(proj, kv, tt=2048)
    return mixer_out_proj(o, proj, DIL_WIDTH // MEM_WIDTH, mem_kv, dil_w_out[i].astype(BF16), x, g_post, tm=512)


def kernel(x, mem, norm_gains, ffn_w_in, ffn_w_out, mem_norm_gain, w_mem_kv, dn_w_in, dn_conv, dn_a_log,
           dn_dt_bias, dn_o_norm, dn_w_out, kv_norm_gain, w_kv, dil_w_in, dil_w_out):
    batch, seq, d = x.shape
    assert batch == 1
    depth = norm_gains.shape[0]
    n_a = dn_w_in.shape[0]
    xs = x.reshape(seq, d)
    mem2 = mem.reshape(mem.shape[1], d)
    mem_kv_all = memory_kv_all_layers(mem2, mem_norm_gain, w_mem_kv, tn=512)
    kv = None
    for l in range(depth):
        gains = norm_gains[l]
        if l == n_a:
            kv = prenorm_matmul(xs, kv_norm_gain, w_kv, tm=1024, tn=1024)
        xs = ffn(xs, gains[0], gains[1], ffn_w_in, ffn_w_out, (l, 0), tm=1024)
        mem_kv = mem_kv_all[l]
        if l < n_a:
            xs = _deltanet_layer(xs, gains[2], gains[3], mem_kv, dn_w_in, l, dn_conv[l], dn_a_log[l],
                                 dn_dt_bias[l], dn_o_norm[l], dn_w_out)
        else:
            i = l - n_a
            xs = _dilated_layer(xs, gains[2], gains[3], kv, mem_kv, dil_w_in, i, dil_w_out)
        xs = ffn(xs, gains[4], gains[5], ffn_w_in, ffn_w_out, (l, 1), tm=1024)
    return xs.reshape(batch, seq, d)
```
